```python
import math
import jax, jax.numpy as jnp
from jax import lax
import numpy as np

D_MODEL = 1024
BATCH = 2
SEQ = 8192
DEPTH = 2
DEC_BATCH = 128
DEC_SEQ = 1
PAST_LEN = 2048
PAGE_SIZE = 128

N_A_LAYERS = DEPTH // 2
N_B_LAYERS = DEPTH - N_A_LAYERS
HEAD_DIM = 128
WIN_GROUPS = ((128, 1), (512, 4), (2048, 16))
N_WIN_GROUPS = 3
WIN_HEADS = 4
WIN_Q_WIDTH = N_WIN_GROUPS * WIN_HEADS * HEAD_DIM
WIN_BLOCK = 128
ALIBI_MAX_BIAS = 8.0
SB_HEADS = 8
SB_WIDTH = SB_HEADS * HEAD_DIM
SB_BLOCK = 128
SB_BIAS_INIT = -7.0
MEM_TOKENS = 256
MEM_HEADS = 4
MEM_WIDTH = MEM_HEADS * HEAD_DIM
IN_A_WIDTH = 3 * WIN_Q_WIDTH + MEM_WIDTH
OUT_A_WIDTH = WIN_HEADS * HEAD_DIM + MEM_WIDTH
IN_B_WIDTH = SB_WIDTH + MEM_WIDTH
OUT_B_WIDTH = SB_WIDTH + MEM_WIDTH
MOE_GROUPS = 4
EXPERTS_PER_GROUP = 8
N_EXPERTS = MOE_GROUPS * EXPERTS_PER_GROUP
TOP_K_INNER = 2
D_EXPERT = 512
MOE_BLOCK = 128
RMS_EPS = 1e-6

kernel_name = "yoco_dilated_stickbreak_hmoe_step"


def _rmsnorm(x, g):
    xf = x.astype(jnp.float32)
    y = xf * lax.rsqrt(jnp.mean(xf * xf, axis=-1, keepdims=True) + RMS_EPS)
    return (y * g.astype(jnp.float32)).astype(x.dtype)


def _alibi_slopes():
    n = N_WIN_GROUPS * WIN_HEADS
    i = jnp.arange(1, n + 1, dtype=jnp.float32)
    return jnp.exp2(-ALIBI_MAX_BIAS * i / n).reshape(N_WIN_GROUPS, WIN_HEADS)


def _softmax_stats(scores):
    m = jnp.max(scores, axis=-1, keepdims=True)
    p = jnp.exp(scores - m)
    den = jnp.sum(p, axis=-1, keepdims=True)
    return p / den, (m + jnp.log(den))[..., 0]


def _dilated_prompt(q, k, v, window, dilation, slopes):
    b, s, h, dh = q.shape
    n_back = window // dilation
    sub_len = s // dilation
    nb = -(-sub_len // WIN_BLOCK)
    lp = nb * WIN_BLOCK

    def split(t):
        t = t.reshape(b, sub_len, dilation, h, dh).transpose(0, 2, 1, 3, 4).reshape(b * dilation, sub_len, h, dh)
        t = jnp.pad(t, ((0, 0), (0, lp - sub_len), (0, 0), (0, 0)))
        return t.reshape(b * dilation, nb, WIN_BLOCK, h, dh)

    def with_prev(t):
        prev = jnp.pad(t[:, :-1], ((0, 0), (1, 0), (0, 0), (0, 0), (0, 0)))
        return jnp.concatenate([prev, t], axis=2)

    qb = split(q)
    kc, vc = with_prev(split(k)), with_prev(split(v))
    scores = jnp.einsum('znqhd,znkhd->znhqk', qb, kc, preferred_element_type=jnp.float32) * (dh ** -0.5)
    qi = jnp.arange(WIN_BLOCK)[:, None]
    kj = jnp.arange(2 * WIN_BLOCK)[None, :]
    rel = qi + WIN_BLOCK - kj
    real_key = (jnp.arange(nb)[:, None, None] > 0) | (kj >= WIN_BLOCK)[None]
    valid = (rel >= 0) & (rel <= n_back) & real_key
    bias = -(slopes.astype(jnp.float32) * dilation)[:, None, None] * rel.astype(jnp.float32)
    scores = jnp.where(valid[None, :, None], scores + bias[None, None], -jnp.inf)
    p, lse = _softmax_stats(scores)
    o = jnp.einsum('znhqk,znkhd->znqhd', p, vc.astype(jnp.float32))
    o = o.reshape(b, dilation, lp, h, dh)[:, :, :sub_len]
    o = jnp.swapaxes(o, 1, 2).reshape(b, s, h, dh)
    lse = jnp.swapaxes(lse, 2, 3).reshape(b, dilation, lp, h)[:, :, :sub_len]
    lse = jnp.swapaxes(lse, 1, 2).reshape(b, s, h)
    return o, lse


def _dilated_step(q, k_new, v_new, k_buf, v_buf, window, dilation, slopes):
    t = q.shape[1]
    lb = k_buf.shape[1]
    kc = jnp.concatenate([k_buf, k_new], axis=1)
    vc = jnp.concatenate([v_buf, v_new], axis=1)
    steps = jnp.arange(window // dilation + 1)
    idx = lb + jnp.arange(t)[:, None] - dilation * steps[None, :]
    valid = idx >= 0
    idx = jnp.maximum(idx, 0)
    kg, vg = kc[:, idx], vc[:, idx]
    scores = jnp.einsum('bqhd,bqkhd->bhqk', q, kg, preferred_element_type=jnp.float32) * (HEAD_DIM ** -0.5)
    scores = scores - (slopes.astype(jnp.float32) * dilation)[None, :, None, None] * steps.astype(jnp.float32)
    scores = jnp.where(valid[None, None], scores, -jnp.inf)
    p, lse = _softmax_stats(scores)
    o = jnp.einsum('bhqk,bqkhd->bqhd', p, vg.astype(jnp.float32))
    keep = min(window, lb + t)
    return o, jnp.swapaxes(lse, 1, 2), kc[:, -keep:], vc[:, -keep:]


def _mix_groups(outs, lses):
    w = jax.nn.softmax(jnp.stack(lses, axis=0), axis=0)
    return jnp.sum(w[..., None] * jnp.stack(outs, axis=0), axis=0)


def _mem_attention(q, mk, mv):
    s = jnp.einsum('bqhd,bmhd->bhqm', q, mk, preferred_element_type=jnp.float32) * (HEAD_DIM ** -0.5)
    p = jax.nn.softmax(s, axis=-1)
    return jnp.einsum('bhqm,bmhd->bqhd', p, mv.astype(jnp.float32)).astype(q.dtype)


def _stick_breaking(q, k, v, q_pos, k_pos, bias):
    z = jnp.einsum('bqhd,bkhd->bhqk', q, k, preferred_element_type=jnp.float32) * (HEAD_DIM ** -0.5)
    z = z + bias.astype(jnp.float32)[None, :, None, None]
    mask = k_pos[None, :] < q_pos[:, None]
    log_keep = jnp.where(mask, jax.nn.log_sigmoid(-z), 0.0)
    after = lax.cumsum(log_keep, axis=3, reverse=True) - log_keep
    a = jnp.where(mask, jnp.exp(jax.nn.log_sigmoid(z) + after), 0.0)
    return jnp.einsum('bhqk,bkhd->bqhd', a, v.astype(jnp.float32)).astype(q.dtype)


def _sb_prompt(q, k, v, bias):
    b, s, h, dh = q.shape
    nb = s // SB_BLOCK
    qb = jnp.moveaxis(q.reshape(b, nb, SB_BLOCK, h, dh), 1, 0)
    k_pos = jnp.arange(s)

    def one_block(args):
        q_blk, i = args
        return _stick_breaking(q_blk, k, v, i * SB_BLOCK + jnp.arange(SB_BLOCK), k_pos, bias)

    ob = lax.map(one_block, (qb, jnp.arange(nb)))
    return jnp.moveaxis(ob, 0, 1).reshape(b, s, h, dh)


def _split_a(p):
    b, t, _ = p.shape
    qa, ka, va, qm = jnp.split(p, [WIN_Q_WIDTH, 2 * WIN_Q_WIDTH, 3 * WIN_Q_WIDTH], axis=-1)
    grp = lambda z: z.reshape(b, t, N_WIN_GROUPS, WIN_HEADS, HEAD_DIM)
    return grp(qa), grp(ka), grp(va), qm.reshape(b, t, MEM_HEADS, HEAD_DIM)


def _shared_kv(x, g, w):
    b, t, _ = x.shape
    k, v = jnp.split(_rmsnorm(x, g) @ w, 2, axis=-1)
    return k.reshape(b, t, SB_HEADS, HEAD_DIM), v.reshape(b, t, SB_HEADS, HEAD_DIM)


def _mem_kv(mem, g, wk, wv):
    b, m, _ = mem.shape
    mn = _rmsnorm(mem, g)
    return (mn @ wk).reshape(b, m, MEM_HEADS, HEAD_DIM), (mn @ wv).reshape(b, m, MEM_HEADS, HEAD_DIM)


def _hier_moe(x, wrg, brg, wre, bre, wg, wu, wd):
    n, d = x.shape
    lg = jnp.dot(x, wrg, preferred_element_type=jnp.float32) + brg.astype(jnp.float32)
    top_g = jnp.argmax(lg, axis=-1)
    p_top = jnp.take_along_axis(jax.nn.softmax(lg, axis=-1), top_g[:, None], axis=-1)
    le = (jnp.dot(x, wre, preferred_element_type=jnp.float32) + bre.astype(jnp.float32))
    le = jnp.take_along_axis(le.reshape(n, MOE_GROUPS, EXPERTS_PER_GROUP), top_g[:, None, None], axis=1)[:, 0]
    vals, idx = lax.top_k(le, TOP_K_INNER)
    gates = p_top * jax.nn.softmax(vals, axis=-1)
    eid = top_g[:, None] * EXPERTS_PER_GROUP + idx
    n_assign = n * TOP_K_INNER
    e_flat = eid.reshape(n_assign)
    g_flat = gates.reshape(n_assign)
    tok = jnp.arange(n_assign) // TOP_K_INNER
    order = jnp.argsort(e_flat, stable=True)
    e_s, tok_s, g_s = e_flat[order], tok[order], g_flat[order]
    counts = jnp.bincount(e_flat, length=N_EXPERTS)
    start = jnp.cumsum(counts) - counts
    padded = (counts + MOE_BLOCK - 1) // MOE_BLOCK * MOE_BLOCK
    pend = jnp.cumsum(padded)
    pstart = pend - padded
    dest = pstart[e_s] + jnp.arange(n_assign) - start[e_s]
    n_blk = (n_assign + N_EXPERTS * (MOE_BLOCK - 1) + MOE_BLOCK - 1) // MOE_BLOCK
    rows = n_blk * MOE_BLOCK
    row_tok = jnp.full((rows,), n, jnp.int32).at[dest].set(tok_s.astype(jnp.int32))
    row_gate = jnp.zeros((rows,), jnp.float32).at[dest].set(g_s)
    blk_start = jnp.arange(n_blk) * MOE_BLOCK
    blk_exp = jnp.minimum(jnp.sum(blk_start[:, None] >= pend[None, :], axis=1), N_EXPERTS - 1)
    x_pad = jnp.concatenate([x, jnp.zeros((1, d), x.dtype)], axis=0)
    xr = x_pad[row_tok].reshape(n_blk, MOE_BLOCK, d)

    def expert_block(args):
        xb, e = args
        return (jax.nn.silu(xb @ wg[e]) * (xb @ wu[e])) @ wd[e]

    yr = lax.map(expert_block, (xr, blk_exp)).reshape(rows, d)
    y = jax.ops.segment_sum(yr * row_gate[:, None].astype(yr.dtype), row_tok, num_segments=n + 1)
    return y[:n]


def setup_inputs(seed: int = 0) -> dict:
    key = jax.random.key(seed)
    keys = iter(jax.random.split(key, 48))
    nrm = lambda shape, scale: jax.random.normal(next(keys), shape, jnp.float32) * scale
    gain = lambda shape: 1.0 + 0.02 * jax.random.normal(next(keys), shape, jnp.float32)
    n_pages = PAST_LEN // PAGE_SIZE
    n_pool = (DEC_BATCH * n_pages * 5 + 3) // 4
    inp = {}
    inp["x_prompt"] = nrm((BATCH, SEQ, D_MODEL), 1.0)
    inp["x_sample"] = nrm((DEC_BATCH, DEC_SEQ, D_MODEL), 1.0)
    inp["mem_prompt"] = nrm((BATCH, MEM_TOKENS, D_MODEL), 1.0)
    for gi, (win, _) in enumerate(WIN_GROUPS):
        lb = min(win, PAST_LEN)
        inp["cache_win_k%d" % gi] = nrm((N_A_LAYERS, DEC_BATCH, lb, WIN_HEADS, HEAD_DIM), 1.0)
        inp["cache_win_v%d" % gi] = nrm((N_A_LAYERS, DEC_BATCH, lb, WIN_HEADS, HEAD_DIM), 1.0)
    inp["cache_mem_k"] = nrm((DEPTH, DEC_BATCH, MEM_TOKENS, MEM_HEADS, HEAD_DIM), 1.0)
    inp["cache_mem_v"] = nrm((DEPTH, DEC_BATCH, MEM_TOKENS, MEM_HEADS, HEAD_DIM), 1.0)
    inp["cache_k_pages"] = nrm((n_pool, PAGE_SIZE, SB_HEADS, HEAD_DIM), 1.0)
    inp["cache_v_pages"] = nrm((n_pool, PAGE_SIZE, SB_HEADS, HEAD_DIM), 1.0)
    perm = jax.random.permutation(next(keys), n_pool)
    inp["page_table"] = perm[:DEC_BATCH * n_pages].reshape(DEC_BATCH, n_pages).astype(jnp.int32)
    inp["g_mix"] = gain((DEPTH, D_MODEL))
    inp["g_mem"] = gain((DEPTH, D_MODEL))
    inp["g_ffn"] = gain((DEPTH, D_MODEL))
    inp["g_kv"] = gain((D_MODEL,))
    inp["g_final"] = gain((D_MODEL,))
    inp["w_in_a"] = nrm((N_A_LAYERS, D_MODEL, IN_A_WIDTH), D_MODEL ** -0.5)
    inp["w_out_a"] = nrm((N_A_LAYERS, OUT_A_WIDTH, D_MODEL), OUT_A_WIDTH ** -0.5)
    inp["w_in_b"] = nrm((N_B_LAYERS, D_MODEL, IN_B_WIDTH), D_MODEL ** -0.5)
    inp["w_out_b"] = nrm((N_B_LAYERS, OUT_B_WIDTH, D_MODEL), OUT_B_WIDTH ** -0.5)
    inp["b_sb"] = SB_BIAS_INIT + nrm((N_B_LAYERS, SB_HEADS), 0.1)
    inp["w_kv"] = nrm((D_MODEL, 2 * SB_WIDTH), D_MODEL ** -0.5)
    inp["w_mem_k"] = nrm((DEPTH, D_MODEL, MEM_WIDTH), D_MODEL ** -0.5)
    inp["w_mem_v"] = nrm((DEPTH, D_MODEL, MEM_WIDTH), D_MODEL ** -0.5)
    inp["w_router_group"] = nrm((DEPTH, D_MODEL, MOE_GROUPS), D_MODEL ** -0.5)
    inp["b_router_group"] = nrm((DEPTH, MOE_GROUPS), 0.01)
    inp["w_router_expert"] = nrm((DEPTH, D_MODEL, N_EXPERTS), D_MODEL ** -0.5)
    inp["b_router_expert"] = nrm((DEPTH, N_EXPERTS), 0.01)
    inp["w_exp_gate"] = nrm((DEPTH, N_EXPERTS, D_MODEL, D_EXPERT), D_MODEL ** -0.5)
    inp["w_exp_up"] = nrm((DEPTH, N_EXPERTS, D_MODEL, D_EXPERT), D_MODEL ** -0.5)
    inp["w_exp_down"] = nrm((DEPTH, N_EXPERTS, D_EXPERT, D_MODEL), D_EXPERT ** -0.5)
    return inp


def reference(x_prompt, x_sample, mem_prompt,
              cache_win_k0, cache_win_v0, cache_win_k1, cache_win_v1, cache_win_k2, cache_win_v2,
              cache_mem_k, cache_mem_v, cache_k_pages, cache_v_pages, page_table,
              g_mix, g_mem, g_ffn, g_kv, g_final,
              w_in_a, w_out_a, w_in_b, w_out_b, b_sb, w_kv, w_mem_k, w_mem_v,
              w_router_group, b_router_group, w_router_expert, b_router_expert,
              w_exp_gate, w_exp_up, w_exp_down):
    slopes = _alibi_slopes()
    bp, s, d = x_prompt.shape
    bs, t, _ = x_sample.shape
    past = page_table.shape[1] * cache_k_pages.shape[1]
    win_k_cache = (cache_win_k0, cache_win_k1, cache_win_k2)
    win_v_cache = (cache_win_v0, cache_win_v1, cache_win_v2)
    wkp = [[] for _ in WIN_GROUPS]
    wvp = [[] for _ in WIN_GROUPS]
    wks = [[] for _ in WIN_GROUPS]
    wvs = [[] for _ in WIN_GROUPS]
    mkp_list, mvp_list = [], []
    h, hs = x_prompt, x_sample
    for l in range(DEPTH):
        mk_p, mv_p = _mem_kv(mem_prompt, g_mem[l], w_mem_k[l], w_mem_v[l])
        mkp_list.append(mk_p)
        mvp_list.append(mv_p)
        mk_s, mv_s = cache_mem_k[l], cache_mem_v[l]
        if l < N_A_LAYERS:
            qa, ka, va, qm = _split_a(_rmsnorm(h, g_mix[l]) @ w_in_a[l])
            qa_s, ka_s, va_s, qm_s = _split_a(_rmsnorm(hs, g_mix[l]) @ w_in_a[l])
            outs, lses, outs_s, lses_s = [], [], [], []
            for gi, (win, dil) in enumerate(WIN_GROUPS):
                o, lse = _dilated_prompt(qa[:, :, gi], ka[:, :, gi], va[:, :, gi], win, dil, slopes[gi])
                outs.append(o)
                lses.append(lse)
                keep = min(win, s)
                wkp[gi].append(ka[:, -keep:, gi])
                wvp[gi].append(va[:, -keep:, gi])
                o2, lse2, kb_new, vb_new = _dilated_step(qa_s[:, :, gi], ka_s[:, :, gi], va_s[:, :, gi],
                                                         win_k_cache[gi][l], win_v_cache[gi][l], win, dil, slopes[gi])
                outs_s.append(o2)
                lses_s.append(lse2)
                wks[gi].append(kb_new)
                wvs[gi].append(vb_new)
            mix_p = _mix_groups(outs, lses).astype(h.dtype)
            mix_s = _mix_groups(outs_s, lses_s).astype(hs.dtype)
            om_p = _mem_attention(qm, mk_p, mv_p)
            om_s = _mem_attention(qm_s, mk_s, mv_s)
            h = h + jnp.concatenate([mix_p.reshape(bp, s, -1), om_p.reshape(bp, s, -1)], axis=-1) @ w_out_a[l]
            hs = hs + jnp.concatenate([mix_s.reshape(bs, t, -1), om_s.reshape(bs, t, -1)], axis=-1) @ w_out_a[l]
        else:
            if l == N_A_LAYERS:
                kv_k_p, kv_v_p = _shared_kv(h, g_kv, w_kv)
                kv_k_s, kv_v_s = _shared_kv(hs, g_kv, w_kv)
                k_past = cache_k_pages[page_table].reshape(bs, past, SB_HEADS, HEAD_DIM)
                v_past = cache_v_pages[page_table].reshape(bs, past, SB_HEADS, HEAD_DIM)
                k_all_s = jnp.concatenate([k_past, kv_k_s], axis=1)
                v_all_s = jnp.concatenate([v_past, kv_v_s], axis=1)
            lb_ = l - N_A_LAYERS
            qb, qm = jnp.split(_rmsnorm(h, g_mix[l]) @ w_in_b[lb_], [SB_WIDTH], axis=-1)
            qb_s, qm_s = jnp.split(_rmsnorm(hs, g_mix[l]) @ w_in_b[lb_], [SB_WIDTH], axis=-1)
            osb_p = _sb_prompt(qb.reshape(bp, s, SB_HEADS, HEAD_DIM), kv_k_p, kv_v_p, b_sb[lb_])
            osb_s = _stick_breaking(qb_s.reshape(bs, t, SB_HEADS, HEAD_DIM), k_all_s, v_all_s,
                                    past + jnp.arange(t), jnp.arange(past + t), b_sb[lb_])
            om_p = _mem_attention(qm.reshape(bp, s, MEM_HEADS, HEAD_DIM), mk_p, mv_p)
            om_s = _mem_attention(qm_s.reshape(bs, t, MEM_HEADS, HEAD_DIM), mk_s, mv_s)
            h = h + jnp.concatenate([osb_p.reshape(bp, s, -1), om_p.reshape(bp, s, -1)], axis=-1) @ w_out_b[lb_]
            hs = hs + jnp.concatenate([osb_s.reshape(bs, t, -1), om_s.reshape(bs, t, -1)], axis=-1) @ w_out_b[lb_]
        moe_args = (w_router_group[l], b_router_group[l], w_router_expert[l], b_router_expert[l],
                    w_exp_gate[l], w_exp_up[l], w_exp_down[l])
        h = h + _hier_moe(_rmsnorm(h, g_ffn[l]).reshape(bp * s, d), *moe_args).reshape(bp, s, d)
        hs = hs + _hier_moe(_rmsnorm(hs, g_ffn[l]).reshape(bs * t, d), *moe_args).reshape(bs, t, d)
    y_prompt = _rmsnorm(h, g_final)
    y_sample = _rmsnorm(hs, g_final)
    win_k0_p, win_k1_p, win_k2_p = [jnp.stack(z, axis=0) for z in wkp]
    win_v0_p, win_v1_p, win_v2_p = [jnp.stack(z, axis=0) for z in wvp]
    win_k0_s, win_k1_s, win_k2_s = [jnp.stack(z, axis=0) for z in wks]
    win_v0_s, win_v1_s, win_v2_s = [jnp.stack(z, axis=0) for z in wvs]
    mem_k_p = jnp.stack(mkp_list, axis=0)
    mem_v_p = jnp.stack(mvp_list, axis=0)
    return (y_prompt, y_sample,
            win_k0_p, win_v0_p, win_k1_p, win_v1_p, win_k2_p, win_v2_p,
            win_k0_s, win_v0_s, win_k1_s, win_v1_s, win_k2_s, win_v2_s,
            mem_k_p, mem_v_p,
            kv_k_p, kv_v_p, kv_k_s, kv_v_s)
```

```python
import functools
import math

import jax
import jax.numpy as jnp
from jax import lax
from jax.experimental import pallas as pl
from jax.experimental.pallas import tpu as pltpu

HEAD_DIM = 128
LANES = 128
RMS_EPS = 1e-6
WIN_GROUPS = ((128, 1), (512, 4), (2048, 16))
N_WIN_GROUPS = 3
WIN_HEADS = 4
WIN_BLOCK = 128
ALIBI_MAX_BIAS = 8.0
SB_HEADS = 8
MEM_HEADS = 4
MOE_GROUPS = 4
EXPERTS_PER_GROUP = 8
N_EXPERTS = MOE_GROUPS * EXPERTS_PER_GROUP
MOE_ROWS = 256
SB_TILE = 256
SCALE = HEAD_DIM ** -0.5
VMEM_LIMIT = 56 * 1024 * 1024

F32 = jnp.float32
BF16 = jnp.bfloat16


def _alibi_slope(g, h):
    n = N_WIN_GROUPS * WIN_HEADS
    return 2.0 ** (-ALIBI_MAX_BIAS * (g * WIN_HEADS + h + 1) / n)


def _params(*sem):
    return pltpu.CompilerParams(dimension_semantics=sem, vmem_limit_bytes=VMEM_LIMIT)


def _dot_t(a, b):
    return lax.dot_general(a, b, (((1,), (1,)), ((), ())), preferred_element_type=F32)


def _softplus(z):
    return jnp.maximum(z, 0.0) + jnp.log(1.0 + jnp.exp(-jnp.abs(z)))


def _rms_matmul_kernel(x_ref, g_ref, w_ref, o_ref):
    x = x_ref[...]
    ms = jnp.mean(x * x, axis=-1, keepdims=True)
    xn = (x * lax.rsqrt(ms + RMS_EPS) * g_ref[...]).astype(BF16)
    o_ref[...] = jnp.dot(xn, w_ref[...], preferred_element_type=F32)


def _rms_matmul(x, g, w_bf16, tm, tn):
    m, d = x.shape
    n = w_bf16.shape[1]
    assert m % tm == 0 and n % tn == 0
    return pl.pallas_call(
        _rms_matmul_kernel,
        grid=(n // tn, m // tm),
        in_specs=[pl.BlockSpec((tm, d), lambda j, i: (i, 0)),
                  pl.BlockSpec((1, d), lambda j, i: (0, 0)),
                  pl.BlockSpec((d, tn), lambda j, i: (0, j))],
        out_specs=pl.BlockSpec((tm, tn), lambda j, i: (i, j)),
        out_shape=jax.ShapeDtypeStruct((m, n), F32),
        compiler_params=_params("parallel", "parallel"),
        name="rms_matmul",
    )(x, g.reshape(1, d), w_bf16)


def _rms_kernel(x_ref, g_ref, o_ref):
    x = x_ref[...]
    ms = jnp.mean(x * x, axis=-1, keepdims=True)
    o_ref[...] = x * lax.rsqrt(ms + RMS_EPS) * g_ref[...]


def _rms(x, g, tm):
    m, d = x.shape
    return pl.pallas_call(
        _rms_kernel,
        grid=(m // tm,),
        in_specs=[pl.BlockSpec((tm, d), lambda i: (i, 0)), pl.BlockSpec((1, d), lambda i: (0, 0))],
        out_specs=pl.BlockSpec((tm, d), lambda i: (i, 0)),
        out_shape=jax.ShapeDtypeStruct((m, d), F32),
        compiler_params=_params("parallel"),
        name="rms_final",
    )(x, g.reshape(1, d))


def _win_prompt_kernel(q_ref, kp_ref, kc_ref, vp_ref, vc_ref, o_ref, lse_ref, *, slopes, blocks_per_seq):
    first = (pl.program_id(0) % blocks_per_seq) == 0
    blk = WIN_BLOCK
    qi = lax.broadcasted_iota(jnp.int32, (blk, 2 * blk), 0)
    kj = lax.broadcasted_iota(jnp.int32, (blk, 2 * blk), 1)
    rel = qi + blk - kj
    valid = (rel >= 0) & (rel <= blk) & ((kj >= blk) | jnp.logical_not(first))
    relf = rel.astype(F32)
    lane = lax.broadcasted_iota(jnp.int32, (blk, LANES), 1)
    lse_all = jnp.zeros((blk, LANES), F32)
    for h in range(WIN_HEADS):
        hs = slice(h * HEAD_DIM, (h + 1) * HEAD_DIM)
        q = q_ref[:, hs].astype(BF16)
        k = jnp.concatenate([kp_ref[:, hs], kc_ref[:, hs]], axis=0).astype(BF16)
        v = jnp.concatenate([vp_ref[:, hs], vc_ref[:, hs]], axis=0).astype(BF16)
        s = _dot_t(q, k) * SCALE
        s = jnp.where(valid, s - slopes[h] * relf, -jnp.inf)
        m = jnp.max(s, axis=-1, keepdims=True)
        p = jnp.exp(s - m)
        den = jnp.sum(p, axis=-1, keepdims=True)
        o = jnp.dot(p.astype(BF16), v, preferred_element_type=F32)
        o_ref[:, hs] = o / den
        lse_all = jnp.where(lane == h, m + jnp.log(den), lse_all)
    lse_ref[...] = lse_all


def _win_prompt(p_a, g, seq):
    n, width = p_a.shape
    _, dil = WIN_GROUPS[g]
    sub = seq // dil
    assert sub % WIN_BLOCK == 0 and n % seq == 0
    gw = WIN_HEADS * HEAD_DIM
    cb = width // gw
    nq = N_WIN_GROUPS
    pa_r = p_a.reshape(n // dil, dil * width)
    nb = n // dil // WIN_BLOCK
    blocks_per_seq = sub // WIN_BLOCK
    slopes = tuple(_alibi_slope(g, h) * dil for h in range(WIN_HEADS))

    def spec(col, prev):
        if prev:
            return pl.BlockSpec((WIN_BLOCK, gw), lambda ib, r: (jnp.maximum(ib - 1, 0), r * cb + col))
        return pl.BlockSpec((WIN_BLOCK, gw), lambda ib, r: (ib, r * cb + col))

    o, lse = pl.pallas_call(
        functools.partial(_win_prompt_kernel, slopes=slopes, blocks_per_seq=blocks_per_seq),
        grid=(nb, dil),
        in_specs=[spec(g, False), spec(nq + g, True), spec(nq + g, False),
                  spec(2 * nq + g, True), spec(2 * nq + g, False)],
        out_specs=[pl.BlockSpec((WIN_BLOCK, gw), lambda ib, r: (ib, r)),
                   pl.BlockSpec((WIN_BLOCK, LANES), lambda ib, r: (ib, r))],
        out_shape=[jax.ShapeDtypeStruct((n // dil, dil * gw), F32),
                   jax.ShapeDtypeStruct((n // dil, dil * LANES), F32)],
        compiler_params=_params("parallel", "parallel"),
        name="win_prompt_g%d" % g,
    )(pa_r, pa_r, pa_r, pa_r, pa_r)
    return o.reshape(n, gw), lse.reshape(n, LANES)


def _win_step_kernel(pa_ref, k0_ref, v0_ref, k1_ref, v1_ref, k2_ref, v2_ref,
                     o0_ref, o1_ref, o2_ref, l0_ref, l1_ref, l2_ref):
    caches = ((k0_ref, v0_ref, o0_ref, l0_ref), (k1_ref, v1_ref, o1_ref, l1_ref), (k2_ref, v2_ref, o2_ref, l2_ref))
    bb = pa_ref.shape[0]
    gw = WIN_HEADS * HEAD_DIM
    lane = lax.broadcasted_iota(jnp.int32, (bb, LANES), 1)
    steps = (WIN_BLOCK - lax.broadcasted_iota(jnp.int32, (1, WIN_BLOCK, 1), 1)).astype(F32)
    for g, (kc_ref, vc_ref, o_ref, l_ref) in enumerate(caches):
        dil = WIN_GROUPS[g][1]
        lse_all = jnp.zeros((bb, LANES), F32)
        for h in range(WIN_HEADS):
            qs = slice(g * gw + h * HEAD_DIM, g * gw + (h + 1) * HEAD_DIM)
            ks = slice((N_WIN_GROUPS + g) * gw + h * HEAD_DIM, (N_WIN_GROUPS + g) * gw + (h + 1) * HEAD_DIM)
            vs = slice((2 * N_WIN_GROUPS + g) * gw + h * HEAD_DIM, (2 * N_WIN_GROUPS + g) * gw + (h + 1) * HEAD_DIM)
            hs = slice(h * HEAD_DIM, (h + 1) * HEAD_DIM)
            q = pa_ref[:, qs]
            kn = pa_ref[:, ks]
            vn = pa_ref[:, vs]
            kc = kc_ref[:, :, hs]
            vc = vc_ref[:, :, hs]
            s_c = jnp.sum(kc * q[:, None, :], axis=-1, keepdims=True) * SCALE
            s_c = s_c - (_alibi_slope(g, h) * dil) * steps
            s_n = jnp.sum(kn * q, axis=-1, keepdims=True) * SCALE
            m = jnp.maximum(jnp.max(s_c, axis=1), s_n)
            p_c = jnp.exp(s_c - m[:, None, :])
            p_n = jnp.exp(s_n - m)
            den = jnp.sum(p_c, axis=1) + p_n
            p_c = p_c / den[:, None, :]
            p_n = p_n / den
            o = jnp.sum(p_c * vc, axis=1) + p_n * vn
            o_ref[:, hs] = o
            lse_all = jnp.where(lane == h, m + jnp.log(den), lse_all)
        l_ref[...] = lse_all


def _win_step(pa_s, caches):
    nseq, width = pa_s.shape
    gw = WIN_HEADS * HEAD_DIM
    bb = 8
    in_specs = [pl.BlockSpec((bb, width), lambda i: (i, 0))]
    args = [pa_s]
    for g, (kc, vc) in enumerate(caches):
        win, dil = WIN_GROUPS[g]
        assert kc.shape[1] == win, "cached window rows must cover the whole window"
        for c in (kc, vc):
            args.append(c.reshape(nseq, WIN_BLOCK, dil * gw))
            in_specs.append(pl.BlockSpec((bb, WIN_BLOCK, gw), lambda i: (i, 0, 0)))
    out_specs = [pl.BlockSpec((bb, gw), lambda i: (i, 0))] * 3 + [pl.BlockSpec((bb, LANES), lambda i: (i, 0))] * 3
    out_shape = [jax.ShapeDtypeStruct((nseq, gw), F32)] * 3 + [jax.ShapeDtypeStruct((nseq, LANES), F32)] * 3
    outs = pl.pallas_call(
        _win_step_kernel,
        grid=(nseq // bb,),
        in_specs=in_specs,
        out_specs=out_specs,
        out_shape=out_shape,
        compiler_params=_params("parallel"),
        name="win_step",
    )(*args)
    return outs[:3], outs[3:]


def _mem_prompt_kernel(q_ref, mk_ref, mv_ref, o_ref):
    for h in range(MEM_HEADS):
        hs = slice(h * HEAD_DIM, (h + 1) * HEAD_DIM)
        q = q_ref[:, hs].astype(BF16)
        s = _dot_t(q, mk_ref[:, hs].astype(BF16)) * SCALE
        m = jnp.max(s, axis=-1, keepdims=True)
        p = jnp.exp(s - m)
        den = jnp.sum(p, axis=-1, keepdims=True)
        o = jnp.dot(p.astype(BF16), mv_ref[:, hs].astype(BF16), preferred_element_type=F32)
        o_ref[:, hs] = o / den


def _mem_prompt(p, q_col_block, memkv, seq, tm):
    n = p.shape[0]
    mw = MEM_HEADS * HEAD_DIM
    mtok = memkv.shape[0] // (n // seq)
    per_seq = seq // tm
    return pl.pallas_call(
        _mem_prompt_kernel,
        grid=(n // tm,),
        in_specs=[pl.BlockSpec((tm, mw), lambda i: (i, q_col_block)),
                  pl.BlockSpec((mtok, mw), lambda i: (i // per_seq, 0)),
                  pl.BlockSpec((mtok, mw), lambda i: (i // per_seq, 1))],
        out_specs=pl.BlockSpec((tm, mw), lambda i: (i, 0)),
        out_shape=jax.ShapeDtypeStruct((n, mw), F32),
        compiler_params=_params("parallel"),
        name="mem_prompt",
    )(p, memkv, memkv)


def _mem_step_kernel(q_ref, mk_ref, mv_ref, o_ref):
    for h in range(MEM_HEADS):
        hs = slice(h * HEAD_DIM, (h + 1) * HEAD_DIM)
        q = q_ref[:, hs]
        mk = mk_ref[:, :, hs]
        mv = mv_ref[:, :, hs]
        s = jnp.sum(mk * q[:, None, :], axis=-1, keepdims=True) * SCALE
        m = jnp.max(s, axis=1, keepdims=True)
        p = jnp.exp(s - m)
        den = jnp.sum(p, axis=1, keepdims=True)
        o_ref[:, hs] = jnp.sum((p / den) * mv, axis=1)


def _mem_step(p, q_col_block, mk, mv):
    nseq = p.shape[0]
    mw = MEM_HEADS * HEAD_DIM
    mtok = mk.shape[1]
    bb = 8
    return pl.pallas_call(
        _mem_step_kernel,
        grid=(nseq // bb,),
        in_specs=[pl.BlockSpec((bb, mw), lambda i: (i, q_col_block)),
                  pl.BlockSpec((bb, mtok, mw), lambda i: (i, 0, 0)),
                  pl.BlockSpec((bb, mtok, mw), lambda i: (i, 0, 0))],
        out_specs=pl.BlockSpec((bb, mw), lambda i: (i, 0)),
        out_shape=jax.ShapeDtypeStruct((nseq, mw), F32),
        compiler_params=_params("parallel"),
        name="mem_step",
    )(p, mk.reshape(nseq, mtok, mw), mv.reshape(nseq, mtok, mw))


def _out_a_kernel(o0_ref, o1_ref, o2_ref, l0_ref, l1_ref, l2_ref, om_ref, h_ref, w_ref, out_ref):
    l0, l1, l2 = l0_ref[...], l1_ref[...], l2_ref[...]
    mx = jnp.maximum(jnp.maximum(l0, l1), l2)
    e0, e1, e2 = jnp.exp(l0 - mx), jnp.exp(l1 - mx), jnp.exp(l2 - mx)
    den = e0 + e1 + e2
    w0, w1, w2 = e0 / den, e1 / den, e2 / den
    parts = []
    for h in range(WIN_HEADS):
        hs = slice(h * HEAD_DIM, (h + 1) * HEAD_DIM)
        parts.append(w0[:, h:h + 1] * o0_ref[:, hs] + w1[:, h:h + 1] * o1_ref[:, hs] + w2[:, h:h + 1] * o2_ref[:, hs])
    parts.append(om_ref[...])
    cat = jnp.concatenate(parts, axis=1).astype(BF16)
    out_ref[...] = h_ref[...] + jnp.dot(cat, w_ref[...], preferred_element_type=F32)


def _out_a(outs, lses, om, h, w_bf16, tm):
    n, d = h.shape
    gw = WIN_HEADS * HEAD_DIM
    row = lambda c: pl.BlockSpec((tm, c), lambda i: (i, 0))
    return pl.pallas_call(
        _out_a_kernel,
        grid=(n // tm,),
        in_specs=[row(gw)] * 3 + [row(LANES)] * 3 + [row(om.shape[1]), row(d),
                                                    pl.BlockSpec(w_bf16.shape, lambda i: (0, 0))],
        out_specs=row(d),
        out_shape=jax.ShapeDtypeStruct((n, d), F32),
        compiler_params=_params("parallel"),
        name="out_a",
    )(*outs, *lses, om, h, w_bf16)


def _out_b_kernel(a_ref, b_ref, h_ref, w_ref, out_ref):
    ka = a_ref.shape[1]
    acc = jnp.dot(a_ref[...].astype(BF16), w_ref[:ka, :], preferred_element_type=F32)
    acc = acc + jnp.dot(b_ref[...].astype(BF16), w_ref[ka:, :], preferred_element_type=F32)
    out_ref[...] = h_ref[...] + acc


def _out_b(a, b, h, w_bf16, tm):
    n, d = h.shape
    row = lambda c: pl.BlockSpec((tm, c), lambda i: (i, 0))
    return pl.pallas_call(
        _out_b_kernel,
        grid=(n // tm,),
        in_specs=[row(a.shape[1]), row(b.shape[1]), row(d), pl.BlockSpec(w_bf16.shape, lambda i: (0, 0))],
        out_specs=row(d),
        out_shape=jax.ShapeDtypeStruct((n, d), F32),
        compiler_params=_params("parallel"),
        name="out_b",
    )(a, b, h, w_bf16)


def _sb_prompt_kernel(q_ref, k_ref, v_ref, b_ref, o_ref):
    t = SB_TILE
    i = pl.program_id(2)
    bias = b_ref[0, pl.program_id(1)]
    q = (q_ref[...] * SCALE).astype(BF16)
    row = lax.broadcasted_iota(jnp.int32, (t, t), 0)
    col = lax.broadcasted_iota(jnp.int32, (t, t), 1)
    later = (row > col).astype(BF16)
    strict = col < row

    def tile(j, acc, cm, mask):
        start = pl.multiple_of(j * t, t)
        k = k_ref[pl.ds(start, t), :].astype(BF16)
        v = v_ref[pl.ds(start, t), :].astype(BF16)
        z = _dot_t(q, k) + bias
        sp = _softplus(z)
        if mask is not None:
            sp = jnp.where(mask, sp, 0.0)
        hi = sp.astype(BF16)
        lo = (sp - hi.astype(F32)).astype(BF16)
        after = jnp.dot(hi, later, preferred_element_type=F32) + jnp.dot(lo, later, preferred_element_type=F32)
        a = jnp.exp(z - sp - after - cm)
        if mask is not None:
            a = jnp.where(mask, a, 0.0)
        acc = acc + jnp.dot(a.astype(BF16), v, preferred_element_type=F32)
        cm = cm + jnp.sum(sp, axis=-1, keepdims=True)
        return acc, cm

    acc, cm = tile(i, jnp.zeros((t, HEAD_DIM), F32), jnp.zeros((t, 1), F32), strict)
    acc, cm = lax.fori_loop(0, i, lambda jj, c: tile(i - 1 - jj, c[0], c[1], None), (acc, cm))
    o_ref[...] = acc


def _sb_prompt(p_b, kv, bias, seq):
    n = kv.shape[0]
    t = SB_TILE
    nq = seq // t
    return pl.pallas_call(
        _sb_prompt_kernel,
        grid=(n // seq, SB_HEADS, nq),
        in_specs=[pl.BlockSpec((t, HEAD_DIM), lambda b, h, i: (b * nq + i, h)),
                  pl.BlockSpec((seq, HEAD_DIM), lambda b, h, i: (b, h)),
                  pl.BlockSpec((seq, HEAD_DIM), lambda b, h, i: (b, SB_HEADS + h)),
                  pl.BlockSpec(memory_space=pltpu.SMEM)],
        out_specs=pl.BlockSpec((t, HEAD_DIM), lambda b, h, i: (b * nq + i, h)),
        out_shape=jax.ShapeDtypeStruct((n, SB_HEADS * HEAD_DIM), F32),
        compiler_params=_params("parallel", "parallel", "parallel"),
        name="sb_prompt",
    )(p_b, kv, kv, bias.reshape(1, SB_HEADS))


def _sb_step_kernel(pt_ref, q_ref, b_ref, *refs, pages_per_step):
    del pt_ref
    k_refs = refs[:pages_per_step]
    v_refs = refs[pages_per_step:2 * pages_per_step]
    o_ref = refs[2 * pages_per_step]
    acc_ref, cm_ref = refs[2 * pages_per_step + 1:]
    c = pl.program_id(1)
    page = k_refs[0].shape[0]

    @pl.when(c == 0)
    def _():
        acc_ref[...] = jnp.zeros_like(acc_ref)
        cm_ref[...] = jnp.zeros_like(cm_ref)

    q = q_ref[0] * SCALE
    bias = b_ref[...]
    lane = lax.broadcasted_iota(jnp.int32, (page, LANES), 1)
    row = lax.broadcasted_iota(jnp.int32, (page, page), 0)
    col = lax.broadcasted_iota(jnp.int32, (page, page), 1)
    later = (col > row).astype(BF16)
    acc = acc_ref[...]
    cm = cm_ref[...]
    for k_ref, v_ref in zip(k_refs, v_refs):
        kq = k_ref[...] * q
        z = jnp.zeros((page, LANES), F32)
        for h in range(SB_HEADS):
            zh = jnp.sum(kq[:, h * HEAD_DIM:(h + 1) * HEAD_DIM], axis=-1, keepdims=True)
            z = jnp.where(lane == h, zh, z)
        z = z + bias
        sp = _softplus(z)
        hi = sp.astype(BF16)
        lo = (sp - hi.astype(F32)).astype(BF16)
        after = jnp.dot(later, hi, preferred_element_type=F32) + jnp.dot(later, lo, preferred_element_type=F32)
        a = jnp.exp(z - sp - after - cm)
        vv = v_ref[...]
        parts = []
        for h in range(SB_HEADS):
            parts.append(jnp.sum(a[:, h:h + 1] * vv[:, h * HEAD_DIM:(h + 1) * HEAD_DIM], axis=0, keepdims=True))
        acc = acc + jnp.concatenate(parts, axis=1)
        cm = cm + jnp.sum(sp, axis=0, keepdims=True)
    acc_ref[...] = acc
    cm_ref[...] = cm

    @pl.when(c == pl.num_programs(1) - 1)
    def _():
        o_ref[0] = acc


def _sb_step(p_b, k_pages, v_pages, page_table, bias, pages_per_step=8):
    nseq = p_b.shape[0]
    pool, page = k_pages.shape[:2]
    n_pages = page_table.shape[1]
    w = SB_HEADS * HEAD_DIM
    assert n_pages % pages_per_step == 0
    chunks = n_pages // pages_per_step
    kp = k_pages.reshape(pool, page, w)
    vp = v_pages.reshape(pool, page, w)
    bias_row = jnp.zeros((1, LANES), F32).at[0, :SB_HEADS].set(bias)

    def page_spec(u):
        return pl.BlockSpec((None, page, w), lambda s, c, pt: (pt[s, n_pages - 1 - (c * pages_per_step + u)], 0, 0))

    grid_spec = pltpu.PrefetchScalarGridSpec(
        num_scalar_prefetch=1,
        grid=(nseq, chunks),
        in_specs=[pl.BlockSpec((1, 1, w), lambda s, c, pt: (s, 0, 0)),
                  pl.BlockSpec((1, LANES), lambda s, c, pt: (0, 0))]
                 + [page_spec(u) for u in range(pages_per_step)] * 2,
        out_specs=pl.BlockSpec((1, 1, w), lambda s, c, pt: (s, 0, 0)),
        scratch_shapes=[pltpu.VMEM((1, w), F32), pltpu.VMEM((1, LANES), F32)],
    )
    q3 = p_b[:, :w].reshape(nseq, 1, w)
    out = pl.pallas_call(
        functools.partial(_sb_step_kernel, pages_per_step=pages_per_step),
        grid_spec=grid_spec,
        out_shape=jax.ShapeDtypeStruct((nseq, 1, w), F32),
        compiler_params=_params("parallel", "arbitrary"),
        name="sb_step",
    )(page_table, q3, bias_row, *([kp] * pages_per_step), *([vp] * pages_per_step))
    return out.reshape(nseq, w)


def _norm_router_kernel(xp_ref, xs_ref, g_ref, wr_ref, br_ref, xn_ref, lg_ref, *, prompt_tiles):
    x = jnp.where(pl.program_id(0) < prompt_tiles, xp_ref[...], xs_ref[...])
    ms = jnp.mean(x * x, axis=-1, keepdims=True)
    xn = x * lax.rsqrt(ms + RMS_EPS) * g_ref[...]
    xn_ref[...] = xn
    lg_ref[...] = jnp.dot(xn, wr_ref[...], preferred_element_type=F32, precision=lax.Precision.HIGHEST) + br_ref[...]


def _norm_router(h_p, h_s, g, wr, br, tm):
    n_p, d = h_p.shape
    n_s = h_s.shape[0]
    assert n_p % tm == 0 and n_s % tm == 0
    tp, ts = n_p // tm, n_s // tm
    return pl.pallas_call(
        functools.partial(_norm_router_kernel, prompt_tiles=tp),
        grid=(tp + ts,),
        in_specs=[pl.BlockSpec((tm, d), lambda i: (jnp.minimum(i, tp - 1), 0)),
                  pl.BlockSpec((tm, d), lambda i: (jnp.maximum(i - tp, 0), 0)),
                  pl.BlockSpec((1, d), lambda i: (0, 0)),
                  pl.BlockSpec((d, LANES), lambda i: (0, 0)),
                  pl.BlockSpec((1, LANES), lambda i: (0, 0))],
        out_specs=[pl.BlockSpec((tm, d), lambda i: (i, 0)),
                   pl.BlockSpec((tm, LANES), lambda i: (i, 0))],
        out_shape=[jax.ShapeDtypeStruct((n_p + n_s, d), F32), jax.ShapeDtypeStruct((n_p + n_s, LANES), F32)],
        compiler_params=_params("parallel"),
        name="norm_router",
    )(h_p, h_s, g.reshape(1, d), wr, br)


def _expert_kernel(be_ref, tok_ref, nu_ref, xn_hbm, rg_ref, wg_ref, wu_ref, wd_ref, yr_ref,
                   xbuf, sem, wg_s, wu_s, wd_s):
    i = pl.program_id(0)
    n_used = nu_ref[0]
    rows = MOE_ROWS

    def row_copy(blk, slot, r):
        tok = tok_ref[blk * rows + r]
        return pltpu.make_async_copy(xn_hbm.at[pl.ds(tok, 1)], xbuf.at[slot, pl.ds(r, 1)], sem.at[slot])

    def start_gather(blk, slot):
        def body(r, carry):
            row_copy(blk, slot, r).start()
            return carry
        lax.fori_loop(0, rows, body, 0)

    def wait_gather(blk, slot):
        def body(r, carry):
            row_copy(blk, slot, r).wait()
            return carry
        lax.fori_loop(0, rows, body, 0)

    @pl.when(jnp.logical_and(i == 0, n_used > 0))
    def _():
        start_gather(0, 0)

    @pl.when(i + 1 < n_used)
    def _():
        start_gather(i + 1, (i + 1) % 2)

    @pl.when(i >= n_used)
    def _():
        yr_ref[...] = jnp.zeros_like(yr_ref)

    @pl.when(i < n_used)
    def _():
        slot = i % 2
        changed = jnp.logical_or(i == 0, be_ref[i] != be_ref[jnp.maximum(i - 1, 0)])

        @pl.when(changed)
        def _():
            wg_s[...] = wg_ref[...].astype(BF16)
            wu_s[...] = wu_ref[...].astype(BF16)
            wd_s[...] = wd_ref[...].astype(BF16)

        wait_gather(i, slot)
        x = xbuf[slot].astype(BF16)
        gate = jnp.dot(x, wg_s[...], preferred_element_type=F32)
        up = jnp.dot(x, wu_s[...], preferred_element_type=F32)
        mid = (gate * jax.nn.sigmoid(gate) * up).astype(BF16)
        y = jnp.dot(mid, wd_s[...], preferred_element_type=F32)
        yr_ref[...] = y * rg_ref[...]


def _experts(xn_all, blk_exp, row_tok, n_used, row_gate, wg, wu, wd):
    n_rows = row_tok.shape[0]
    n_blk = n_rows // MOE_ROWS
    d, de = wg.shape[1:]
    grid_spec = pltpu.PrefetchScalarGridSpec(
        num_scalar_prefetch=3,
        grid=(n_blk,),
        in_specs=[pl.BlockSpec(memory_space=pl.ANY),
                  pl.BlockSpec((MOE_ROWS, 1), lambda i, be, tok, nu: (i, 0)),
                  pl.BlockSpec((None, d, de), lambda i, be, tok, nu: (be[i], 0, 0)),
                  pl.BlockSpec((None, d, de), lambda i, be, tok, nu: (be[i], 0, 0)),
                  pl.BlockSpec((None, de, d), lambda i, be, tok, nu: (be[i], 0, 0))],
        out_specs=pl.BlockSpec((MOE_ROWS, d), lambda i, be, tok, nu: (i, 0)),
        scratch_shapes=[pltpu.VMEM((2, MOE_ROWS, d), F32), pltpu.SemaphoreType.DMA((2,)),
                        pltpu.VMEM((d, de), BF16), pltpu.VMEM((d, de), BF16), pltpu.VMEM((de, d), BF16)],
    )
    return pl.pallas_call(
        _expert_kernel,
        grid_spec=grid_spec,
        out_shape=jax.ShapeDtypeStruct((n_rows, d), F32),
        compiler_params=_params("arbitrary"),
        name="moe_experts",
    )(blk_exp, row_tok, n_used, xn_all, row_gate.reshape(n_rows, 1), wg, wu, wd)


def _combine_kernel(p0_ref, p1_ref, yr_hbm, h_ref, out_ref, buf, sem, *, tc):
    i = pl.program_id(0)
    nsteps = pl.num_programs(0)

    def row_copy(step, slot, r, which):
        pos_ref = p1_ref if which else p0_ref
        pos = pos_ref[step * tc + r]
        return pltpu.make_async_copy(yr_hbm.at[pl.ds(pos, 1)], buf.at[slot, which, pl.ds(r, 1)], sem.at[slot])

    def start_gather(step, slot):
        def body(r, carry):
            row_copy(step, slot, r, 0).start()
            row_copy(step, slot, r, 1).start()
            return carry
        lax.fori_loop(0, tc, body, 0)

    def wait_gather(step, slot):
        def body(r, carry):
            row_copy(step, slot, r, 0).wait()
            row_copy(step, slot, r, 1).wait()
            return carry
        lax.fori_loop(0, tc, body, 0)

    @pl.when(i == 0)
    def _():
        start_gather(0, 0)

    @pl.when(i + 1 < nsteps)
    def _():
        start_gather(i + 1, (i + 1) % 2)

    slot = i % 2
    wait_gather(i, slot)
    out_ref[...] = h_ref[...] + (buf[slot, 0] + buf[slot, 1])


def _combine(yr, pos0, pos1, h, tc):
    n, d = h.shape
    grid_spec = pltpu.PrefetchScalarGridSpec(
        num_scalar_prefetch=2,
        grid=(n // tc,),
        in_specs=[pl.BlockSpec(memory_space=pl.ANY),
                  pl.BlockSpec((tc, d), lambda i, p0, p1: (i, 0))],
        out_specs=pl.BlockSpec((tc, d), lambda i, p0, p1: (i, 0)),
        scratch_shapes=[pltpu.VMEM((2, 2, tc, d), F32), pltpu.SemaphoreType.DMA((2,))],
    )
    return pl.pallas_call(
        functools.partial(_combine_kernel, tc=tc),
        grid_spec=grid_spec,
        out_shape=jax.ShapeDtypeStruct((n, d), F32),
        compiler_params=_params("arbitrary"),
        name="moe_combine",
    )(pos0, pos1, yr, h)


def _route(logits):
    n = logits.shape[0]
    lg = logits[:, :MOE_GROUPS]
    top_g = jnp.argmax(lg, axis=-1)
    p_top = jnp.take_along_axis(jax.nn.softmax(lg, axis=-1), top_g[:, None], axis=-1)
    le = logits[:, MOE_GROUPS:MOE_GROUPS + N_EXPERTS].reshape(n, MOE_GROUPS, EXPERTS_PER_GROUP)
    le = jnp.take_along_axis(le, top_g[:, None, None], axis=1)[:, 0]
    vals, idx = lax.top_k(le, 2)
    gates = p_top * jax.nn.softmax(vals, axis=-1)
    eid = top_g[:, None] * EXPERTS_PER_GROUP + idx
    return eid.astype(jnp.int32), gates


def _dispatch(eid, gates):
    n = eid.shape[0]
    n_assign = 2 * n
    e_flat = eid.reshape(n_assign)
    onehot = (e_flat[:, None] == jnp.arange(N_EXPERTS, dtype=jnp.int32)[None, :]).astype(jnp.int32)
    csum = jnp.cumsum(onehot, axis=0)
    counts = csum[-1]
    rank = jnp.sum((csum - onehot) * onehot, axis=1)
    padded = (counts + MOE_ROWS - 1) // MOE_ROWS * MOE_ROWS
    pend = jnp.cumsum(padded)
    pstart = pend - padded
    dest = (pstart[e_flat] + rank).astype(jnp.int32)
    n_blk = (n_assign + N_EXPERTS * (MOE_ROWS - 1) + MOE_ROWS - 1) // MOE_ROWS
    rows = n_blk * MOE_ROWS
    tok = jnp.arange(n_assign, dtype=jnp.int32) // 2
    row_tok = jnp.zeros((rows,), jnp.int32).at[dest].set(tok)
    row_gate = jnp.zeros((rows,), F32).at[dest].set(gates.reshape(n_assign))
    blk_start = jnp.arange(n_blk, dtype=jnp.int32) * MOE_ROWS
    blk_exp = jnp.minimum(jnp.sum(blk_start[:, None] >= pend[None, :], axis=1), N_EXPERTS - 1).astype(jnp.int32)
    n_used = (pend[-1] // MOE_ROWS).astype(jnp.int32).reshape(1)
    pos = dest.reshape(n, 2)
    return row_tok, row_gate, blk_exp, n_used, pos[:, 0], pos[:, 1]


def _hier_moe(h_p, h_s, g, wr, br, wg, wu, wd, tm_p, tm_s):
    n_p, d = h_p.shape
    n_s = h_s.shape[0]
    xn_all, logits = _norm_router(h_p, h_s, g, wr, br, math.gcd(128, n_s))
    eid, gates = _route(logits)
    row_tok, row_gate, blk_exp, n_used, pos0, pos1 = _dispatch(eid, gates)
    yr = _experts(xn_all, blk_exp, row_tok, n_used, row_gate, wg, wu, wd)
    out_p = _combine(yr, pos0[:n_p], pos1[:n_p], h_p, 128)
    out_s = _combine(yr, pos0[n_p:], pos1[n_p:], h_s, min(128, n_s))
    return out_p, out_s


def kernel(x_prompt, x_sample, mem_prompt, cache_win_k0, cache_win_v0, cache_win_k1, cache_win_v1, cache_win_k2, cache_win_v2, cache_mem_k, cache_mem_v, cache_k_pages, cache_v_pages, page_table, g_mix, g_mem, g_ffn, g_kv, g_final, w_in_a, w_out_a, w_in_b, w_out_b, b_sb, w_kv, w_mem_k, w_mem_v, w_router_group, b_router_group, w_router_expert, b_router_expert, w_exp_gate, w_exp_up, w_exp_down):
    bp, seq, d = x_prompt.shape
    bs, t_new, _ = x_sample.shape
    assert t_new == 1, "one new token per sample sequence"
    depth = g_mix.shape[0]
    n_a = w_in_a.shape[0]
    n_p = bp * seq
    tm_p = 512
    tm_s = bs
    mem_tok = mem_prompt.shape[1]
    gw = WIN_HEADS * HEAD_DIM
    win_k_cache = (cache_win_k0, cache_win_k1, cache_win_k2)
    win_v_cache = (cache_win_v0, cache_win_v1, cache_win_v2)

    h = x_prompt.reshape(n_p, d)
    hs = x_sample.reshape(bs, d)
    mem2 = mem_prompt.reshape(bp * mem_tok, d)

    wkp = [[] for _ in WIN_GROUPS]
    wvp = [[] for _ in WIN_GROUPS]
    wks = [[] for _ in WIN_GROUPS]
    wvs = [[] for _ in WIN_GROUPS]
    mkp_list, mvp_list = [], []
    kv_p = kv_s = None

    for l in range(depth):
        w_mem = jnp.concatenate([w_mem_k[l], w_mem_v[l]], axis=1).astype(BF16)
        memkv = _rms_matmul(mem2, g_mem[l], w_mem, mem2.shape[0], w_mem.shape[1])
        mw = MEM_HEADS * HEAD_DIM
        mkp_list.append(memkv[:, :mw].reshape(bp, mem_tok, MEM_HEADS, HEAD_DIM))
        mvp_list.append(memkv[:, mw:].reshape(bp, mem_tok, MEM_HEADS, HEAD_DIM))
        if l < n_a:
            w_in = w_in_a[l].astype(BF16)
            pa = _rms_matmul(h, g_mix[l], w_in, tm_p, 2560)
            pa_s = _rms_matmul(hs, g_mix[l], w_in, tm_s, 2560)
            outs, lses = [], []
            for g, (win, dil) in enumerate(WIN_GROUPS):
                o, lse = _win_prompt(pa, g, seq)
                outs.append(o)
                lses.append(lse)
                keep = min(win, seq)
                ka = pa[:, (N_WIN_GROUPS + g) * gw:(N_WIN_GROUPS + g + 1) * gw].reshape(bp, seq, WIN_HEADS, HEAD_DIM)
                va = pa[:, (2 * N_WIN_GROUPS + g) * gw:(2 * N_WIN_GROUPS + g + 1) * gw].reshape(bp, seq, WIN_HEADS, HEAD_DIM)
                wkp[g].append(ka[:, -keep:])
                wvp[g].append(va[:, -keep:])
                kn = pa_s[:, (N_WIN_GROUPS + g) * gw:(N_WIN_GROUPS + g + 1) * gw].reshape(bs, 1, WIN_HEADS, HEAD_DIM)
                vn = pa_s[:, (2 * N_WIN_GROUPS + g) * gw:(2 * N_WIN_GROUPS + g + 1) * gw].reshape(bs, 1, WIN_HEADS, HEAD_DIM)
                keep_s = min(win, win_k_cache[g].shape[2] + 1)
                wks[g].append(jnp.concatenate([win_k_cache[g][l], kn], axis=1)[:, -keep_s:])
                wvs[g].append(jnp.concatenate([win_v_cache[g][l], vn], axis=1)[:, -keep_s:])
            outs_s, lses_s = _win_step(pa_s, [(win_k_cache[g][l], win_v_cache[g][l]) for g in range(N_WIN_GROUPS)])
            q_mem_block = 3 * N_WIN_GROUPS
            om = _mem_prompt(pa, q_mem_block, memkv, seq, tm_p)
            om_s = _mem_step(pa_s, q_mem_block, cache_mem_k[l], cache_mem_v[l])
            w_out = w_out_a[l].astype(BF16)
            h = _out_a(outs, lses, om, h, w_out, tm_p)
            hs = _out_a(outs_s, lses_s, om_s, hs, w_out, tm_s)
        else:
            lb = l - n_a
            if l == n_a:
                wkv = w_kv.astype(BF16)
                kv_p = _rms_matmul(h, g_kv, wkv, tm_p, wkv.shape[1])
                kv_s = _rms_matmul(hs, g_kv, wkv, tm_s, wkv.shape[1])
            w_in = w_in_b[lb].astype(BF16)
            pb = _rms_matmul(h, g_mix[l], w_in, tm_p, w_in.shape[1])
            pb_s = _rms_matmul(hs, g_mix[l], w_in, tm_s, w_in.shape[1])
            osb = _sb_prompt(pb, kv_p, b_sb[lb], seq)
            osb_s = _sb_step(pb_s, cache_k_pages, cache_v_pages, page_table, b_sb[lb])
            q_mem_block = SB_HEADS * HEAD_DIM // (MEM_HEADS * HEAD_DIM)
            om = _mem_prompt(pb, q_mem_block, memkv, seq, tm_p)
            om_s = _mem_step(pb_s, q_mem_block, cache_mem_k[l], cache_mem_v[l])
            w_out = w_out_b[lb].astype(BF16)
            h = _out_b(osb, om, h, w_out, tm_p)
            hs = _out_b(osb_s, om_s, hs, w_out, tm_s)
        wr = jnp.zeros((d, LANES), F32).at[:, :MOE_GROUPS].set(w_router_group[l])
        wr = wr.at[:, MOE_GROUPS:MOE_GROUPS + N_EXPERTS].set(w_router_expert[l])
        br = jnp.zeros((1, LANES), F32).at[0, :MOE_GROUPS].set(b_router_group[l])
        br = br.at[0, MOE_GROUPS:MOE_GROUPS + N_EXPERTS].set(b_router_expert[l])
        h, hs = _hier_moe(h, hs, g_ffn[l], wr, br, w_exp_gate[l], w_exp_up[l], w_exp_down[l], tm_p, tm_s)

    y_prompt = _rms(h, g_final, tm_p).reshape(bp, seq, d)
    y_sample = _rms(hs, g_final, tm_s).reshape(bs, 1, d)
    stack = lambda zs: jnp.stack(zs, axis=0)
    sbw = SB_HEADS * HEAD_DIM
    kv_k_p = kv_p[:, :sbw].reshape(bp, seq, SB_HEADS, HEAD_DIM)
    kv_v_p = kv_p[:, sbw:].reshape(bp, seq, SB_HEADS, HEAD_DIM)
    kv_k_s = kv_s[:, :sbw].reshape(bs, 1, SB_HEADS, HEAD_DIM)
    kv_v_s = kv_s[:, sbw:].reshape(bs, 1, SB_HEADS, HEAD_DIM)
    return (y_prompt, y_sample,
            stack(wkp[0]), stack(wvp[0]), stack(wkp[1]), stack(wvp[1]), stack(wkp[2]), stack(wvp[2]),
            stack(wks[0]), stack(wvs[0]), stack(wks[1]), stack(wvs[1]), stack(wks[2]), stack(wvs[2]),
            stack(mkp_list), stack(mvp_list),
            kv_k_p, kv_v_p, kv_k_s, kv_v_s)
```

```python
import functools
import math

import jax
import jax.numpy as jnp
from jax import lax
from jax.experimental import pallas as pl
from jax.experimental.pallas import tpu as pltpu

HEAD_DIM = 128
LANES = 128
SUBLANES = 8
GATHER_UNROLL = 8
RMS_EPS = 1e-6
WIN_GROUPS = ((128, 1), (512, 4), (2048, 16))
N_WIN_GROUPS = 3
WIN_HEADS = 4
WIN_BLOCK = 128
ALIBI_MAX_BIAS = 8.0
SB_HEADS = 8
MEM_HEADS = 4
MOE_GROUPS = 4
EXPERTS_PER_GROUP = 8
N_EXPERTS = MOE_GROUPS * EXPERTS_PER_GROUP
MOE_ROWS = 256
SB_TILE = 256
SCALE = HEAD_DIM ** -0.5
VMEM_LIMIT = 56 * 1024 * 1024

F32 = jnp.float32
BF16 = jnp.bfloat16


def _alibi_slope(g, h):
    n = N_WIN_GROUPS * WIN_HEADS
    return 2.0 ** (-ALIBI_MAX_BIAS * (g * WIN_HEADS + h + 1) / n)


def _params(*sem):
    return pltpu.CompilerParams(dimension_semantics=sem, vmem_limit_bytes=VMEM_LIMIT)


def _dot_t(a, b):
    return lax.dot_general(a, b, (((1,), (1,)), ((), ())), preferred_element_type=F32)


def _softplus(z):
    return jnp.maximum(z, 0.0) + jnp.log(1.0 + jnp.exp(-jnp.abs(z)))


def _rms_matmul_kernel(x_ref, g_ref, w_ref, o_ref, *maybe_bf16_ref):
    x = x_ref[...]
    ms = jnp.mean(x * x, axis=-1, keepdims=True)
    xn = (x * lax.rsqrt(ms + RMS_EPS) * g_ref[...]).astype(BF16)
    y = jnp.dot(xn, w_ref[...], preferred_element_type=F32)
    o_ref[...] = y
    for r in maybe_bf16_ref:
        r[...] = y.astype(BF16)


def _rms_matmul(x, g, w_bf16, tm, tn, also_bf16=False):
    m, d = x.shape
    n = w_bf16.shape[1]
    assert m % tm == 0 and n % tn == 0
    out_spec = pl.BlockSpec((tm, tn), lambda j, i: (i, j))
    out_specs, out_shape = out_spec, jax.ShapeDtypeStruct((m, n), F32)
    if also_bf16:
        out_specs, out_shape = [out_spec, out_spec], [out_shape, jax.ShapeDtypeStruct((m, n), BF16)]
    return pl.pallas_call(
        _rms_matmul_kernel,
        grid=(n // tn, m // tm),
        in_specs=[pl.BlockSpec((tm, d), lambda j, i: (i, 0)),
                  pl.BlockSpec((1, d), lambda j, i: (0, 0)),
                  pl.BlockSpec((d, tn), lambda j, i: (0, j))],
        out_specs=out_specs,
        out_shape=out_shape,
        compiler_params=_params("parallel", "parallel"),
        name="rms_matmul",
    )(x, g.reshape(1, d), w_bf16)


def _rms_kernel(x_ref, g_ref, o_ref):
    x = x_ref[...]
    ms = jnp.mean(x * x, axis=-1, keepdims=True)
    o_ref[...] = x * lax.rsqrt(ms + RMS_EPS) * g_ref[...]


def _rms(x, g, tm):
    m, d = x.shape
    return pl.pallas_call(
        _rms_kernel,
        grid=(m // tm,),
        in_specs=[pl.BlockSpec((tm, d), lambda i: (i, 0)), pl.BlockSpec((1, d), lambda i: (0, 0))],
        out_specs=pl.BlockSpec((tm, d), lambda i: (i, 0)),
        out_shape=jax.ShapeDtypeStruct((m, d), F32),
        compiler_params=_params("parallel"),
        name="rms_final",
    )(x, g.reshape(1, d))


def _win_prompt_kernel(q_ref, kp_ref, kc_ref, vp_ref, vc_ref, o_ref, lse_ref, *, slopes, blocks_per_seq):
    first = (pl.program_id(0) % blocks_per_seq) == 0
    blk = WIN_BLOCK
    qi = lax.broadcasted_iota(jnp.int32, (blk, 2 * blk), 0)
    kj = lax.broadcasted_iota(jnp.int32, (blk, 2 * blk), 1)
    rel = qi + blk - kj
    valid = (rel >= 0) & (rel <= blk) & ((kj >= blk) | jnp.logical_not(first))
    relf = rel.astype(F32)
    lane = lax.broadcasted_iota(jnp.int32, (blk, LANES), 1)
    lse_all = jnp.zeros((blk, LANES), F32)
    for h in range(WIN_HEADS):
        hs = slice(h * HEAD_DIM, (h + 1) * HEAD_DIM)
        q = q_ref[:, hs].astype(BF16)
        k = jnp.concatenate([kp_ref[:, hs], kc_ref[:, hs]], axis=0).astype(BF16)
        v = jnp.concatenate([vp_ref[:, hs], vc_ref[:, hs]], axis=0).astype(BF16)
        s = _dot_t(q, k) * SCALE
        s = jnp.where(valid, s - slopes[h] * relf, -jnp.inf)
        m = jnp.max(s, axis=-1, keepdims=True)
        p = jnp.exp(s - m)
        den = jnp.sum(p, axis=-1, keepdims=True)
        o = jnp.dot(p.astype(BF16), v, preferred_element_type=F32)
        o_ref[:, hs] = o / den
        lse_all = jnp.where(lane == h, m + jnp.log(den), lse_all)
    lse_ref[...] = lse_all


def _win_prompt(p_a, g, seq):
    n, width = p_a.shape
    _, dil = WIN_GROUPS[g]
    sub = seq // dil
    assert sub % WIN_BLOCK == 0 and n % seq == 0
    gw = WIN_HEADS * HEAD_DIM
    cb = width // gw
    nq = N_WIN_GROUPS
    pa_r = p_a.reshape(n // dil, dil * width)
    nb = n // dil // WIN_BLOCK
    blocks_per_seq = sub // WIN_BLOCK
    slopes = tuple(_alibi_slope(g, h) * dil for h in range(WIN_HEADS))

    def spec(col, prev):
        if prev:
            return pl.BlockSpec((WIN_BLOCK, gw), lambda ib, r: (jnp.maximum(ib - 1, 0), r * cb + col))
        return pl.BlockSpec((WIN_BLOCK, gw), lambda ib, r: (ib, r * cb + col))

    o, lse = pl.pallas_call(
        functools.partial(_win_prompt_kernel, slopes=slopes, blocks_per_seq=blocks_per_seq),
        grid=(nb, dil),
        in_specs=[spec(g, False), spec(nq + g, True), spec(nq + g, False),
                  spec(2 * nq + g, True), spec(2 * nq + g, False)],
        out_specs=[pl.BlockSpec((WIN_BLOCK, gw), lambda ib, r: (ib, r)),
                   pl.BlockSpec((WIN_BLOCK, LANES), lambda ib, r: (ib, r))],
        out_shape=[jax.ShapeDtypeStruct((n // dil, dil * gw), F32),
                   jax.ShapeDtypeStruct((n // dil, dil * LANES), F32)],
        compiler_params=_params("parallel", "parallel"),
        name="win_prompt_g%d" % g,
    )(pa_r, pa_r, pa_r, pa_r, pa_r)
    return o.reshape(n, gw), lse.reshape(n, LANES)


def _win_step_kernel(q_ref, kn_ref, vn_ref, kc_ref, vc_ref, o_ref, l_ref, *, g):
    bb = q_ref.shape[0]
    dil = WIN_GROUPS[g][1]
    lane = lax.broadcasted_iota(jnp.int32, (bb, LANES), 1)
    steps = (WIN_BLOCK - lax.broadcasted_iota(jnp.int32, (1, WIN_BLOCK, 1), 1)).astype(F32)
    lse_all = jnp.zeros((bb, LANES), F32)
    for h in range(WIN_HEADS):
        hs = slice(h * HEAD_DIM, (h + 1) * HEAD_DIM)
        q = q_ref[:, hs]
        kn = kn_ref[:, hs]
        vn = vn_ref[:, hs]
        kc = kc_ref[:, :, h, :]
        vc = vc_ref[:, :, h, :]
        s_c = jnp.sum(kc * q[:, None, :], axis=-1, keepdims=True) * SCALE
        s_c = s_c - (_alibi_slope(g, h) * dil) * steps
        s_n = jnp.sum(kn * q, axis=-1, keepdims=True) * SCALE
        m = jnp.maximum(jnp.max(s_c, axis=1), s_n)
        p_c = jnp.exp(s_c - m[:, None, :])
        p_n = jnp.exp(s_n - m)
        den = jnp.sum(p_c, axis=1) + p_n
        p_c = p_c / den[:, None, :]
        p_n = p_n / den
        o_ref[:, hs] = jnp.sum(p_c * vc, axis=1) + p_n * vn
        lse_all = jnp.where(lane == h, m + jnp.log(den), lse_all)
    l_ref[...] = lse_all


def _win_step(pa_s, g, k_cache, v_cache, layer):
    nseq, width = pa_s.shape
    gw = WIN_HEADS * HEAD_DIM
    win, dil = WIN_GROUPS[g]
    assert k_cache.shape[2] == win, "cached window rows must cover the whole window"
    bb = 8
    col = lambda c: pl.BlockSpec((bb, gw), lambda i: (i, c))
    cache_spec = pl.BlockSpec((None, bb, WIN_BLOCK, None, WIN_HEADS, HEAD_DIM), lambda i: (layer, i, 0, 0, 0, 0))
    split = lambda c: c.reshape(c.shape[0], nseq, WIN_BLOCK, dil, WIN_HEADS, HEAD_DIM)
    return pl.pallas_call(
        functools.partial(_win_step_kernel, g=g),
        grid=(nseq // bb,),
        in_specs=[col(g), col(N_WIN_GROUPS + g), col(2 * N_WIN_GROUPS + g), cache_spec, cache_spec],
        out_specs=[pl.BlockSpec((bb, gw), lambda i: (i, 0)), pl.BlockSpec((bb, LANES), lambda i: (i, 0))],
        out_shape=[jax.ShapeDtypeStruct((nseq, gw), F32), jax.ShapeDtypeStruct((nseq, LANES), F32)],
        compiler_params=_params("parallel"),
        name="win_step_g%d" % g,
    )(pa_s, pa_s, pa_s, split(k_cache), split(v_cache))


def _mem_prompt_kernel(q_ref, mk_ref, mv_ref, o_ref):
    for h in range(MEM_HEADS):
        hs = slice(h * HEAD_DIM, (h + 1) * HEAD_DIM)
        q = q_ref[:, hs].astype(BF16)
        s = _dot_t(q, mk_ref[:, hs].astype(BF16)) * SCALE
        m = jnp.max(s, axis=-1, keepdims=True)
        p = jnp.exp(s - m)
        den = jnp.sum(p, axis=-1, keepdims=True)
        o = jnp.dot(p.astype(BF16), mv_ref[:, hs].astype(BF16), preferred_element_type=F32)
        o_ref[:, hs] = o / den


def _mem_prompt(p, q_col_block, memkv, seq, tm):
    n = p.shape[0]
    mw = MEM_HEADS * HEAD_DIM
    mtok = memkv.shape[0] // (n // seq)
    per_seq = seq // tm
    return pl.pallas_call(
        _mem_prompt_kernel,
        grid=(n // tm,),
        in_specs=[pl.BlockSpec((tm, mw), lambda i: (i, q_col_block)),
                  pl.BlockSpec((mtok, mw), lambda i: (i // per_seq, 0)),
                  pl.BlockSpec((mtok, mw), lambda i: (i // per_seq, 1))],
        out_specs=pl.BlockSpec((tm, mw), lambda i: (i, 0)),
        out_shape=jax.ShapeDtypeStruct((n, mw), F32),
        compiler_params=_params("parallel"),
        name="mem_prompt",
    )(p, memkv, memkv)


def _mem_step_kernel(q_ref, mk_ref, mv_ref, o_ref):
    for h in range(MEM_HEADS):
        hs = slice(h * HEAD_DIM, (h + 1) * HEAD_DIM)
        q = q_ref[:, hs]
        mk = mk_ref[:, :, h, :]
        mv = mv_ref[:, :, h, :]
        s = jnp.sum(mk * q[:, None, :], axis=-1, keepdims=True) * SCALE
        m = jnp.max(s, axis=1, keepdims=True)
        p = jnp.exp(s - m)
        den = jnp.sum(p, axis=1, keepdims=True)
        o_ref[:, hs] = jnp.sum((p / den) * mv, axis=1)


def _mem_step(p, q_col_block, mk, mv, layer):
    nseq = p.shape[0]
    mw = MEM_HEADS * HEAD_DIM
    mtok = mk.shape[2]
    bb = 8
    cache_spec = pl.BlockSpec((None, bb, mtok, MEM_HEADS, HEAD_DIM), lambda i: (layer, i, 0, 0, 0))
    return pl.pallas_call(
        _mem_step_kernel,
        grid=(nseq // bb,),
        in_specs=[pl.BlockSpec((bb, mw), lambda i: (i, q_col_block)), cache_spec, cache_spec],
        out_specs=pl.BlockSpec((bb, mw), lambda i: (i, 0)),
        out_shape=jax.ShapeDtypeStruct((nseq, mw), F32),
        compiler_params=_params("parallel"),
        name="mem_step",
    )(p, mk, mv)


def _out_a_kernel(o0_ref, o1_ref, o2_ref, l0_ref, l1_ref, l2_ref, om_ref, h_ref, w_ref, out_ref):
    l0, l1, l2 = l0_ref[...], l1_ref[...], l2_ref[...]
    mx = jnp.maximum(jnp.maximum(l0, l1), l2)
    e0, e1, e2 = jnp.exp(l0 - mx), jnp.exp(l1 - mx), jnp.exp(l2 - mx)
    den = e0 + e1 + e2
    w0, w1, w2 = e0 / den, e1 / den, e2 / den
    parts = []
    for h in range(WIN_HEADS):
        hs = slice(h * HEAD_DIM, (h + 1) * HEAD_DIM)
        parts.append(w0[:, h:h + 1] * o0_ref[:, hs] + w1[:, h:h + 1] * o1_ref[:, hs] + w2[:, h:h + 1] * o2_ref[:, hs])
    parts.append(om_ref[...])
    cat = jnp.concatenate(parts, axis=1).astype(BF16)
    out_ref[...] = h_ref[...] + jnp.dot(cat, w_ref[...], preferred_element_type=F32)


def _out_a(outs, lses, om, h, w_bf16, tm):
    n, d = h.shape
    gw = WIN_HEADS * HEAD_DIM
    row = lambda c: pl.BlockSpec((tm, c), lambda i: (i, 0))
    return pl.pallas_call(
        _out_a_kernel,
        grid=(n // tm,),
        in_specs=[row(gw)] * 3 + [row(LANES)] * 3 + [row(om.shape[1]), row(d),
                                                    pl.BlockSpec(w_bf16.shape, lambda i: (0, 0))],
        out_specs=row(d),
        out_shape=jax.ShapeDtypeStruct((n, d), F32),
        compiler_params=_params("parallel"),
        name="out_a",
    )(*outs, *lses, om, h, w_bf16)


def _out_b_kernel(a_ref, b_ref, h_ref, w_ref, out_ref):
    ka = a_ref.shape[1]
    acc = jnp.dot(a_ref[...].astype(BF16), w_ref[:ka, :], preferred_element_type=F32)
    acc = acc + jnp.dot(b_ref[...].astype(BF16), w_ref[ka:, :], preferred_element_type=F32)
    out_ref[...] = h_ref[...] + acc


def _out_b(a, b, h, w_bf16, tm):
    n, d = h.shape
    row = lambda c: pl.BlockSpec((tm, c), lambda i: (i, 0))
    return pl.pallas_call(
        _out_b_kernel,
        grid=(n // tm,),
        in_specs=[row(a.shape[1]), row(b.shape[1]), row(d), pl.BlockSpec(w_bf16.shape, lambda i: (0, 0))],
        out_specs=row(d),
        out_shape=jax.ShapeDtypeStruct((n, d), F32),
        compiler_params=_params("parallel"),
        name="out_b",
    )(a, b, h, w_bf16)


def _sb_prompt_kernel(q_ref, k_ref, v_ref, b_ref, o_ref):
    t = SB_TILE
    i = pl.program_id(2)
    heads = q_ref.shape[1] // HEAD_DIM
    head0 = pl.program_id(1) * heads
    row = lax.broadcasted_iota(jnp.int32, (t, t), 0)
    col = lax.broadcasted_iota(jnp.int32, (t, t), 1)
    later = (row > col).astype(BF16)
    strict = col < row
    cols = [slice(h * HEAD_DIM, (h + 1) * HEAD_DIM) for h in range(heads)]
    qs = [(q_ref[:, c] * SCALE).astype(BF16) for c in cols]
    biases = [b_ref[0, head0 + h] for h in range(heads)]

    def run(tiles, carry):
        units = [(h, pl.multiple_of(j * t, t), mask) for h in range(heads) for j, mask in tiles]
        zs = [_dot_t(qs[h], k_ref[pl.ds(s, t), cols[h]]) + biases[h] for h, s, _ in units]
        sps = []
        for (_, _, mask), z in zip(units, zs):
            sp = _softplus(z)
            sps.append(sp if mask is None else jnp.where(mask, sp, 0.0))
        afters = []
        for sp in sps:
            hi = sp.astype(BF16)
            lo = (sp - hi.astype(F32)).astype(BF16)
            afters.append(jnp.dot(hi, later, preferred_element_type=F32)
                          + jnp.dot(lo, later, preferred_element_type=F32))
        carry = list(carry)
        for (h, s, mask), z, sp, after in zip(units, zs, sps, afters):
            acc, cm = carry[h]
            a = jnp.exp(z - sp - after - cm)
            if mask is not None:
                a = jnp.where(mask, a, 0.0)
            acc = acc + jnp.dot(a.astype(BF16), v_ref[pl.ds(s, t), cols[h]], preferred_element_type=F32)
            carry[h] = (acc, cm + jnp.sum(sp, axis=-1, keepdims=True))
        return tuple(carry)

    has_partner = (i % 2) == 1
    partner_mask = row < jnp.where(has_partner, t, 0)
    init = tuple((jnp.zeros((t, HEAD_DIM), F32), jnp.zeros((t, 1), F32)) for _ in range(heads))
    carry = run([(i, strict), (jnp.maximum(i - 1, 0), partner_mask)], init)
    rest = i - has_partner.astype(jnp.int32)

    def pair(p, c):
        j = rest - 1 - 2 * p
        return run([(j, None), (j - 1, None)], c)

    carry = lax.fori_loop(0, rest // 2, pair, carry)
    for h in range(heads):
        o_ref[:, cols[h]] = carry[h][0]


def _sb_prompt(p_b, kv_bf16, bias, seq, heads_per_step=2):
    n = kv_bf16.shape[0]
    t = SB_TILE
    nq = seq // t
    hw = heads_per_step * HEAD_DIM
    groups = SB_HEADS // heads_per_step
    return pl.pallas_call(
        _sb_prompt_kernel,
        grid=(n // seq, groups, nq),
        in_specs=[pl.BlockSpec((t, hw), lambda b, h, i: (b * nq + i, h)),
                  pl.BlockSpec((seq, hw), lambda b, h, i: (b, h)),
                  pl.BlockSpec((seq, hw), lambda b, h, i: (b, groups + h)),
                  pl.BlockSpec(memory_space=pltpu.SMEM)],
        out_specs=pl.BlockSpec((t, hw), lambda b, h, i: (b * nq + i, h)),
        out_shape=jax.ShapeDtypeStruct((n, SB_HEADS * HEAD_DIM), F32),
        compiler_params=_params("parallel", "parallel", "parallel"),
        name="sb_prompt",
    )(p_b, kv_bf16, kv_bf16, bias.reshape(1, SB_HEADS))


def _sb_step_kernel(pt_ref, q_ref, b_ref, *refs, pages_per_step):
    del pt_ref
    k_refs = refs[:pages_per_step]
    v_refs = refs[pages_per_step:2 * pages_per_step]
    o_ref = refs[2 * pages_per_step]
    acc_ref, cm_ref = refs[2 * pages_per_step + 1:]
    c = pl.program_id(1)
    page = k_refs[0].shape[0]

    @pl.when(c == 0)
    def _():
        acc_ref[...] = jnp.zeros_like(acc_ref)
        cm_ref[...] = jnp.zeros_like(cm_ref)

    q = q_ref[0] * SCALE
    bias = b_ref[...]
    lane = lax.broadcasted_iota(jnp.int32, (page, LANES), 1)
    row = lax.broadcasted_iota(jnp.int32, (page, page), 0)
    col = lax.broadcasted_iota(jnp.int32, (page, page), 1)
    later = (col > row).astype(BF16)
    acc = acc_ref[...]
    cm = cm_ref[...]
    for k_ref, v_ref in zip(k_refs, v_refs):
        z = jnp.zeros((page, LANES), F32)
        for h in range(SB_HEADS):
            zh = jnp.sum(k_ref[:, h, :] * q[:, h * HEAD_DIM:(h + 1) * HEAD_DIM], axis=-1, keepdims=True)
            z = jnp.where(lane == h, zh, z)
        z = z + bias
        sp = _softplus(z)
        hi = sp.astype(BF16)
        lo = (sp - hi.astype(F32)).astype(BF16)
        after = jnp.dot(later, hi, preferred_element_type=F32) + jnp.dot(later, lo, preferred_element_type=F32)
        a = jnp.exp(z - sp - after - cm)
        parts = []
        for h in range(SB_HEADS):
            parts.append(jnp.sum(a[:, h:h + 1] * v_ref[:, h, :], axis=0, keepdims=True))
        acc = acc + jnp.concatenate(parts, axis=1)
        cm = cm + jnp.sum(sp, axis=0, keepdims=True)
    acc_ref[...] = acc
    cm_ref[...] = cm

    @pl.when(c == pl.num_programs(1) - 1)
    def _():
        o_ref[0] = acc


def _sb_step(p_b, k_pages, v_pages, page_table, bias, pages_per_step=8):
    nseq = p_b.shape[0]
    page = k_pages.shape[1]
    n_pages = page_table.shape[1]
    w = SB_HEADS * HEAD_DIM
    assert n_pages % pages_per_step == 0
    chunks = n_pages // pages_per_step
    kp, vp = k_pages, v_pages
    bias_row = jnp.zeros((1, LANES), F32).at[0, :SB_HEADS].set(bias)

    def page_spec(u):
        return pl.BlockSpec((None, page, SB_HEADS, HEAD_DIM),
                            lambda s, c, pt: (pt[s, n_pages - 1 - (c * pages_per_step + u)], 0, 0, 0))

    grid_spec = pltpu.PrefetchScalarGridSpec(
        num_scalar_prefetch=1,
        grid=(nseq, chunks),
        in_specs=[pl.BlockSpec((1, 1, w), lambda s, c, pt: (s, 0, 0)),
                  pl.BlockSpec((1, LANES), lambda s, c, pt: (0, 0))]
                 + [page_spec(u) for u in range(pages_per_step)] * 2,
        out_specs=pl.BlockSpec((1, 1, w), lambda s, c, pt: (s, 0, 0)),
        scratch_shapes=[pltpu.VMEM((1, w), F32), pltpu.VMEM((1, LANES), F32)],
    )
    q3 = p_b[:, :w].reshape(nseq, 1, w)
    out = pl.pallas_call(
        functools.partial(_sb_step_kernel, pages_per_step=pages_per_step),
        grid_spec=grid_spec,
        out_shape=jax.ShapeDtypeStruct((nseq, 1, w), F32),
        compiler_params=_params("parallel", "arbitrary"),
        name="sb_step",
    )(page_table, q3, bias_row, *([kp] * pages_per_step), *([vp] * pages_per_step))
    return out.reshape(nseq, w)


def _store_row_tiles(ref, val):
    rows = val.shape[0]
    for s in range(SUBLANES):
        ref[pl.ds(s, rows, stride=SUBLANES), :] = val[:, s * LANES:(s + 1) * LANES]


def _load_row_tiles(ref):
    rows = ref.shape[0] // SUBLANES
    return jnp.concatenate([ref[pl.ds(s, rows, stride=SUBLANES), :] for s in range(SUBLANES)], axis=1)


def _row_tile(ref, r):
    return ref.at[pl.ds(pl.multiple_of(r * SUBLANES, SUBLANES), SUBLANES), :]


def _norm_router_kernel(xp_ref, xs_ref, g_ref, wr_ref, br_ref, xn_ref, lg_ref, *, prompt_tiles):
    x = jnp.where(pl.program_id(0) < prompt_tiles, xp_ref[...], xs_ref[...])
    ms = jnp.mean(x * x, axis=-1, keepdims=True)
    xn = x * lax.rsqrt(ms + RMS_EPS) * g_ref[...]
    _store_row_tiles(xn_ref, xn)
    lg_ref[...] = jnp.dot(xn, wr_ref[...], preferred_element_type=F32, precision=lax.Precision.HIGHEST) + br_ref[...]


def _norm_router(h_p, h_s, g, wr, br, tm):
    n_p, d = h_p.shape
    n_s = h_s.shape[0]
    assert n_p % tm == 0 and n_s % tm == 0 and d == SUBLANES * LANES
    tp, ts = n_p // tm, n_s // tm
    return pl.pallas_call(
        functools.partial(_norm_router_kernel, prompt_tiles=tp),
        grid=(tp + ts,),
        in_specs=[pl.BlockSpec((tm, d), lambda i: (jnp.minimum(i, tp - 1), 0)),
                  pl.BlockSpec((tm, d), lambda i: (jnp.maximum(i - tp, 0), 0)),
                  pl.BlockSpec((1, d), lambda i: (0, 0)),
                  pl.BlockSpec((d, LANES), lambda i: (0, 0)),
                  pl.BlockSpec((1, LANES), lambda i: (0, 0))],
        out_specs=[pl.BlockSpec((tm * SUBLANES, LANES), lambda i: (i, 0)),
                   pl.BlockSpec((tm, LANES), lambda i: (i, 0))],
        out_shape=[jax.ShapeDtypeStruct(((n_p + n_s) * SUBLANES, LANES), F32),
                   jax.ShapeDtypeStruct((n_p + n_s, LANES), F32)],
        compiler_params=_params("parallel"),
        name="norm_router",
    )(h_p, h_s, g.reshape(1, d), wr, br)


def _expert_kernel(be_ref, tok_ref, nu_ref, xn_hbm, rg_ref, wg_ref, wu_ref, wd_ref, yr_ref,
                   xbuf, sem, wg_s, wu_s, wd_s):
    i = pl.program_id(0)
    n_used = nu_ref[0]
    rows = MOE_ROWS

    def row_copy(blk, slot, r):
        tok = tok_ref[blk * rows + r]
        return pltpu.make_async_copy(_row_tile(xn_hbm, tok), _row_tile(xbuf.at[slot], r), sem.at[slot])

    def start_gather(blk, slot):
        def body(r, carry):
            row_copy(blk, slot, r).start()
            return carry
        lax.fori_loop(0, rows, body, 0, unroll=GATHER_UNROLL)

    def wait_gather(blk, slot):
        def body(r, carry):
            row_copy(blk, slot, r).wait()
            return carry
        lax.fori_loop(0, rows, body, 0, unroll=GATHER_UNROLL)

    @pl.when(jnp.logical_and(i == 0, n_used > 0))
    def _():
        start_gather(0, 0)

    @pl.when(i + 1 < n_used)
    def _():
        start_gather(i + 1, (i + 1) % 2)

    @pl.when(i >= n_used)
    def _():
        yr_ref[...] = jnp.zeros_like(yr_ref)

    @pl.when(i < n_used)
    def _():
        slot = i % 2
        changed = jnp.logical_or(i == 0, be_ref[i] != be_ref[jnp.maximum(i - 1, 0)])

        @pl.when(changed)
        def _():
            wg_s[...] = wg_ref[...].astype(BF16)
            wu_s[...] = wu_ref[...].astype(BF16)
            wd_s[...] = wd_ref[...].astype(BF16)

        wait_gather(i, slot)
        x = _load_row_tiles(xbuf.at[slot]).astype(BF16)
        gate = jnp.dot(x, wg_s[...], preferred_element_type=F32)
        up = jnp.dot(x, wu_s[...], preferred_element_type=F32)
        mid = (gate * jax.nn.sigmoid(gate) * up).astype(BF16)
        y = jnp.dot(mid, wd_s[...], preferred_element_type=F32)
        _store_row_tiles(yr_ref, y * rg_ref[...])


def _experts(xn_all, blk_exp, row_tok, n_used, row_gate, wg, wu, wd):
    n_rows = row_tok.shape[0]
    n_blk = n_rows // MOE_ROWS
    d, de = wg.shape[1:]
    grid_spec = pltpu.PrefetchScalarGridSpec(
        num_scalar_prefetch=3,
        grid=(n_blk,),
        in_specs=[pl.BlockSpec(memory_space=pl.ANY),
                  pl.BlockSpec((MOE_ROWS, 1), lambda i, be, tok, nu: (i, 0)),
                  pl.BlockSpec((None, d, de), lambda i, be, tok, nu: (be[i], 0, 0)),
                  pl.BlockSpec((None, d, de), lambda i, be, tok, nu: (be[i], 0, 0)),
                  pl.BlockSpec((None, de, d), lambda i, be, tok, nu: (be[i], 0, 0))],
        out_specs=pl.BlockSpec((MOE_ROWS * SUBLANES, LANES), lambda i, be, tok, nu: (i, 0)),
        scratch_shapes=[pltpu.VMEM((2, MOE_ROWS * SUBLANES, LANES), F32), pltpu.SemaphoreType.DMA((2,)),
                        pltpu.VMEM((d, de), BF16), pltpu.VMEM((d, de), BF16), pltpu.VMEM((de, d), BF16)],
    )
    return pl.pallas_call(
        _expert_kernel,
        grid_spec=grid_spec,
        out_shape=jax.ShapeDtypeStruct((n_rows * SUBLANES, LANES), F32),
        compiler_params=_params("arbitrary"),
        name="moe_experts",
    )(blk_exp, row_tok, n_used, xn_all, row_gate.reshape(n_rows, 1), wg, wu, wd)


def _combine_kernel(p0_ref, p1_ref, yr_hbm, h_ref, out_ref, buf, sem, *, tc):
    i = pl.program_id(0)
    nsteps = pl.num_programs(0)

    def row_copy(step, slot, r, which):
        pos_ref = p1_ref if which else p0_ref
        pos = pos_ref[step * tc + r]
        return pltpu.make_async_copy(_row_tile(yr_hbm, pos), _row_tile(buf.at[slot, which], r), sem.at[slot])

    def start_gather(step, slot):
        def body(r, carry):
            row_copy(step, slot, r, 0).start()
            row_copy(step, slot, r, 1).start()
            return carry
        lax.fori_loop(0, tc, body, 0, unroll=GATHER_UNROLL)

    def wait_gather(step, slot):
        def body(r, carry):
            row_copy(step, slot, r, 0).wait()
            row_copy(step, slot, r, 1).wait()
            return carry
        lax.fori_loop(0, tc, body, 0, unroll=GATHER_UNROLL)

    @pl.when(i == 0)
    def _():
        start_gather(0, 0)

    @pl.when(i + 1 < nsteps)
    def _():
        start_gather(i + 1, (i + 1) % 2)

    slot = i % 2
    wait_gather(i, slot)
    out_ref[...] = h_ref[...] + (_load_row_tiles(buf.at[slot, 0]) + _load_row_tiles(buf.at[slot, 1]))


def _combine(yr, pos0, pos1, h, tc):
    n, d = h.shape
    grid_spec = pltpu.PrefetchScalarGridSpec(
        num_scalar_prefetch=2,
        grid=(n // tc,),
        in_specs=[pl.BlockSpec(memory_space=pl.ANY),
                  pl.BlockSpec((tc, d), lambda i, p0, p1: (i, 0))],
        out_specs=pl.BlockSpec((tc, d), lambda i, p0, p1: (i, 0)),
        scratch_shapes=[pltpu.VMEM((2, 2, tc * SUBLANES, LANES), F32), pltpu.SemaphoreType.DMA((2,))],
    )
    return pl.pallas_call(
        functools.partial(_combine_kernel, tc=tc),
        grid_spec=grid_spec,
        out_shape=jax.ShapeDtypeStruct((n, d), F32),
        compiler_params=_params("arbitrary"),
        name="moe_combine",
    )(pos0, pos1, yr, h)


def _route(logits):
    n = logits.shape[0]
    lg = logits[:, :MOE_GROUPS]
    top_g = jnp.argmax(lg, axis=-1)
    p_top = jnp.take_along_axis(jax.nn.softmax(lg, axis=-1), top_g[:, None], axis=-1)
    le = logits[:, MOE_GROUPS:MOE_GROUPS + N_EXPERTS].reshape(n, MOE_GROUPS, EXPERTS_PER_GROUP)
    le = jnp.take_along_axis(le, top_g[:, None, None], axis=1)[:, 0]
    vals, idx = lax.top_k(le, 2)
    gates = p_top * jax.nn.softmax(vals, axis=-1)
    eid = top_g[:, None] * EXPERTS_PER_GROUP + idx
    return eid.astype(jnp.int32), gates


def _dispatch(eid, gates):
    n = eid.shape[0]
    n_assign = 2 * n
    e_flat = eid.reshape(n_assign)
    onehot = (e_flat[:, None] == jnp.arange(N_EXPERTS, dtype=jnp.int32)[None, :]).astype(jnp.int32)
    csum = jnp.cumsum(onehot, axis=0)
    counts = csum[-1]
    rank = jnp.sum((csum - onehot) * onehot, axis=1)
    padded = (counts + MOE_ROWS - 1) // MOE_ROWS * MOE_ROWS
    pend = jnp.cumsum(padded)
    pstart = pend - padded
    dest = (pstart[e_flat] + rank).astype(jnp.int32)
    n_blk = (n_assign + N_EXPERTS * (MOE_ROWS - 1) + MOE_ROWS - 1) // MOE_ROWS
    rows = n_blk * MOE_ROWS
    blk_start = jnp.arange(n_blk, dtype=jnp.int32) * MOE_ROWS
    blk_exp = jnp.minimum(jnp.sum(blk_start[:, None] >= pend[None, :], axis=1), N_EXPERTS - 1).astype(jnp.int32)
    n_used = (pend[-1] // MOE_ROWS).astype(jnp.int32).reshape(1)
    order = jnp.argsort(e_flat, stable=True).astype(jnp.int32)
    start = jnp.cumsum(counts) - counts
    row = jnp.arange(rows, dtype=jnp.int32)
    e_row = blk_exp[row // MOE_ROWS]
    idx = row - pstart[e_row]
    valid = idx < counts[e_row]
    src = order[jnp.clip(start[e_row] + idx, 0, n_assign - 1)]
    row_tok = jnp.where(valid, src // 2, 0).astype(jnp.int32)
    row_gate = jnp.where(valid, gates.reshape(n_assign)[src], 0.0)
    pos = dest.reshape(n, 2)
    return row_tok, row_gate, blk_exp, n_used, pos[:, 0], pos[:, 1]


def _hier_moe(h_p, h_s, g, wr, br, wg, wu, wd, tm_p, tm_s):
    n_p, d = h_p.shape
    n_s = h_s.shape[0]
    xn_all, logits = _norm_router(h_p, h_s, g, wr, br, math.gcd(128, n_s))
    eid, gates = _route(logits)
    row_tok, row_gate, blk_exp, n_used, pos0, pos1 = _dispatch(eid, gates)
    yr = _experts(xn_all, blk_exp, row_tok, n_used, row_gate, wg, wu, wd)
    out_p = _combine(yr, pos0[:n_p], pos1[:n_p], h_p, 128)
    out_s = _combine(yr, pos0[n_p:], pos1[n_p:], h_s, min(128, n_s))
    return out_p, out_s


def kernel(x_prompt, x_sample, mem_prompt, cache_win_k0, cache_win_v0, cache_win_k1, cache_win_v1, cache_win_k2, cache_win_v2, cache_mem_k, cache_mem_v, cache_k_pages, cache_v_pages, page_table, g_mix, g_mem, g_ffn, g_kv, g_final, w_in_a, w_out_a, w_in_b, w_out_b, b_sb, w_kv, w_mem_k, w_mem_v, w_router_group, b_router_group, w_router_expert, b_router_expert, w_exp_gate, w_exp_up, w_exp_down):
    bp, seq, d = x_prompt.shape
    bs, t_new, _ = x_sample.shape
    assert t_new == 1, "one new token per sample sequence"
    depth = g_mix.shape[0]
    n_a = w_in_a.shape[0]
    n_p = bp * seq
    tm_p = 512
    tm_s = bs
    mem_tok = mem_prompt.shape[1]
    gw = WIN_HEADS * HEAD_DIM
    win_k_cache = (cache_win_k0, cache_win_k1, cache_win_k2)
    win_v_cache = (cache_win_v0, cache_win_v1, cache_win_v2)

    h = x_prompt.reshape(n_p, d)
    hs = x_sample.reshape(bs, d)
    mem2 = mem_prompt.reshape(bp * mem_tok, d)

    wkp = [[] for _ in WIN_GROUPS]
    wvp = [[] for _ in WIN_GROUPS]
    wks = [[] for _ in WIN_GROUPS]
    wvs = [[] for _ in WIN_GROUPS]
    mkp_list, mvp_list = [], []
    kv_p = kv_s = None

    for l in range(depth):
        w_mem = jnp.concatenate([w_mem_k[l], w_mem_v[l]], axis=1).astype(BF16)
        memkv = _rms_matmul(mem2, g_mem[l], w_mem, mem2.shape[0], w_mem.shape[1])
        mw = MEM_HEADS * HEAD_DIM
        mkp_list.append(memkv[:, :mw].reshape(bp, mem_tok, MEM_HEADS, HEAD_DIM))
        mvp_list.append(memkv[:, mw:].reshape(bp, mem_tok, MEM_HEADS, HEAD_DIM))
        if l < n_a:
            w_in = w_in_a[l].astype(BF16)
            pa = _rms_matmul(h, g_mix[l], w_in, tm_p, 2560)
            pa_s = _rms_matmul(hs, g_mix[l], w_in, tm_s, 2560)
            outs, lses = [], []
            for g, (win, dil) in enumerate(WIN_GROUPS):
                o, lse = _win_prompt(pa, g, seq)
                outs.append(o)
                lses.append(lse)
                keep = min(win, seq)
                ka = pa[:, (N_WIN_GROUPS + g) * gw:(N_WIN_GROUPS + g + 1) * gw].reshape(bp, seq, WIN_HEADS, HEAD_DIM)
                va = pa[:, (2 * N_WIN_GROUPS + g) * gw:(2 * N_WIN_GROUPS + g + 1) * gw].reshape(bp, seq, WIN_HEADS, HEAD_DIM)
                wkp[g].append(ka[:, -keep:])
                wvp[g].append(va[:, -keep:])
                kn = pa_s[:, (N_WIN_GROUPS + g) * gw:(N_WIN_GROUPS + g + 1) * gw].reshape(bs, 1, WIN_HEADS, HEAD_DIM)
                vn = pa_s[:, (2 * N_WIN_GROUPS + g) * gw:(2 * N_WIN_GROUPS + g + 1) * gw].reshape(bs, 1, WIN_HEADS, HEAD_DIM)
                keep_s = min(win, win_k_cache[g].shape[2] + 1)
                wks[g].append(jnp.concatenate([win_k_cache[g][l], kn], axis=1)[:, -keep_s:])
                wvs[g].append(jnp.concatenate([win_v_cache[g][l], vn], axis=1)[:, -keep_s:])
            outs_s, lses_s = zip(*[_win_step(pa_s, g, win_k_cache[g], win_v_cache[g], l)
                                   for g in range(N_WIN_GROUPS)])
            q_mem_block = 3 * N_WIN_GROUPS
            om = _mem_prompt(pa, q_mem_block, memkv, seq, tm_p)
            om_s = _mem_step(pa_s, q_mem_block, cache_mem_k, cache_mem_v, l)
            w_out = w_out_a[l].astype(BF16)
            h = _out_a(outs, lses, om, h, w_out, tm_p)
            hs = _out_a(outs_s, lses_s, om_s, hs, w_out, tm_s)
        else:
            lb = l - n_a
            if l == n_a:
                wkv = w_kv.astype(BF16)
                kv_p, kv_p16 = _rms_matmul(h, g_kv, wkv, tm_p, wkv.shape[1], also_bf16=True)
                kv_s = _rms_matmul(hs, g_kv, wkv, tm_s, wkv.shape[1])
            w_in = w_in_b[lb].astype(BF16)
            pb = _rms_matmul(h, g_mix[l], w_in, tm_p, w_in.shape[1])
            pb_s = _rms_matmul(hs, g_mix[l], w_in, tm_s, w_in.shape[1])
            osb = _sb_prompt(pb, kv_p16, b_sb[lb], seq)
            osb_s = _sb_step(pb_s, cache_k_pages, cache_v_pages, page_table, b_sb[lb])
            q_mem_block = SB_HEADS * HEAD_DIM // (MEM_HEADS * HEAD_DIM)
            om = _mem_prompt(pb, q_mem_block, memkv, seq, tm_p)
            om_s = _mem_step(pb_s, q_mem_block, cache_mem_k, cache_mem_v, l)
            w_out = w_out_b[lb].astype(BF16)
            h = _out_b(osb, om, h, w_out, tm_p)
            hs = _out_b(osb_s, om_s, hs, w_out, tm_s)
        wr = jnp.zeros((d, LANES), F32).at[:, :MOE_GROUPS].set(w_router_group[l])
        wr = wr.at[:, MOE_GROUPS:MOE_GROUPS + N_EXPERTS].set(w_router_expert[l])
        br = jnp.zeros((1, LANES), F32).at[0, :MOE_GROUPS].set(b_router_group[l])
        br = br.at[0, MOE_GROUPS:MOE_GROUPS + N_EXPERTS].set(b_router_expert[l])
        h, hs = _hier_moe(h, hs, g_ffn[l], wr, br, w_exp_gate[l], w_exp_up[l], w_exp_down[l], tm_p, tm_s)

    y_prompt = _rms(h, g_final, tm_p).reshape(bp, seq, d)
    y_sample = _rms(hs, g_final, tm_s).reshape(bs, 1, d)
    stack = lambda zs: jnp.stack(zs, axis=0)
    sbw = SB_HEADS * HEAD_DIM
    kv_k_p = kv_p[:, :sbw].reshape(bp, seq, SB_HEADS, HEAD_DIM)
    kv_v_p = kv_p[:, sbw:].reshape(bp, seq, SB_HEADS, HEAD_DIM)
    kv_k_s = kv_s[:, :sbw].reshape(bs, 1, SB_HEADS, HEAD_DIM)
    kv_v_s = kv_s[:, sbw:].reshape(bs, 1, SB_HEADS, HEAD_DIM)
    return (y_prompt, y_sample,
            stack(wkp[0]), stack(wvp[0]), stack(wkp[1]), stack(wvp[1]), stack(wkp[2]), stack(wvp[2]),
            stack(wks[0]), stack(wvs[0]), stack(wks[1]), stack(wvs[1]), stack(wks[2]), stack(wvs[2]),
            stack(mkp_list), stack(mvp_list),
            kv_k_p, kv_v_p, kv_k_s, kv_v_s)
```

```python
import functools
import math

import jax
import jax.numpy as jnp
from jax import lax
from jax.experimental import pallas as pl
from jax.experimental.pallas import tpu as pltpu

HEAD_DIM = 128
LANES = 128
SUBLANES = 8
GATHER_UNROLL = 8
RMS_EPS = 1e-6
WIN_GROUPS = ((128, 1), (512, 4), (2048, 16))
N_WIN_GROUPS = 3
WIN_HEADS = 4
WIN_BLOCK = 128
ALIBI_MAX_BIAS = 8.0
SB_HEADS = 8
MEM_HEADS = 4
MOE_GROUPS = 4
EXPERTS_PER_GROUP = 8
N_EXPERTS = MOE_GROUPS * EXPERTS_PER_GROUP
MOE_ROWS = 256
SB_TILE = 256
SCALE = HEAD_DIM ** -0.5
VMEM_LIMIT = 56 * 1024 * 1024

F32 = jnp.float32
BF16 = jnp.bfloat16


def _alibi_slope(g, h):
    n = N_WIN_GROUPS * WIN_HEADS
    return 2.0 ** (-ALIBI_MAX_BIAS * (g * WIN_HEADS + h + 1) / n)


def _params(*sem):
    return pltpu.CompilerParams(dimension_semantics=sem, vmem_limit_bytes=VMEM_LIMIT)


def _dot_t(a, b):
    return lax.dot_general(a, b, (((1,), (1,)), ((), ())), preferred_element_type=F32)


def _softplus(z):
    return jnp.maximum(z, 0.0) + jnp.log(1.0 + jnp.exp(-jnp.abs(z)))


def _rms_matmul_kernel(x_ref, g_ref, w_ref, o_ref, *maybe_bf16_ref):
    x = x_ref[...]
    ms = jnp.mean(x * x, axis=-1, keepdims=True)
    xn = (x * lax.rsqrt(ms + RMS_EPS) * g_ref[...]).astype(BF16)
    y = jnp.dot(xn, w_ref[...], preferred_element_type=F32)
    o_ref[...] = y
    for r in maybe_bf16_ref:
        r[...] = y.astype(BF16)


def _rms_matmul(x, g, w_bf16, tm, tn, also_bf16=False):
    m, d = x.shape
    n = w_bf16.shape[1]
    assert m % tm == 0 and n % tn == 0
    out_spec = pl.BlockSpec((tm, tn), lambda j, i: (i, j))
    out_specs, out_shape = out_spec, jax.ShapeDtypeStruct((m, n), F32)
    if also_bf16:
        out_specs, out_shape = [out_spec, out_spec], [out_shape, jax.ShapeDtypeStruct((m, n), BF16)]
    return pl.pallas_call(
        _rms_matmul_kernel,
        grid=(n // tn, m // tm),
        in_specs=[pl.BlockSpec((tm, d), lambda j, i: (i, 0)),
                  pl.BlockSpec((1, d), lambda j, i: (0, 0)),
                  pl.BlockSpec((d, tn), lambda j, i: (0, j))],
        out_specs=out_specs,
        out_shape=out_shape,
        compiler_params=_params("parallel", "parallel"),
        name="rms_matmul",
    )(x, g.reshape(1, d), w_bf16)


def _kv_proj_kernel(x_ref, g_ref, w_ref, k_ref, v_ref, kv16_ref):
    x = x_ref[...]
    ms = jnp.mean(x * x, axis=-1, keepdims=True)
    xn = (x * lax.rsqrt(ms + RMS_EPS) * g_ref[...]).astype(BF16)
    y = jnp.dot(xn, w_ref[...], preferred_element_type=F32)
    half = y.shape[1] // 2
    _store_row_tiles(k_ref, y[:, :half])
    _store_row_tiles(v_ref, y[:, half:])
    kv16_ref[...] = y.astype(BF16)


def _kv_proj(x, g, w_bf16, tm):
    m, d = x.shape
    n = w_bf16.shape[1]
    assert m % tm == 0 and n == 2 * SB_HEADS * HEAD_DIM and SB_HEADS == SUBLANES
    tiles = pl.BlockSpec((tm * SUBLANES, LANES), lambda i: (i, 0))
    return pl.pallas_call(
        _kv_proj_kernel,
        grid=(m // tm,),
        in_specs=[pl.BlockSpec((tm, d), lambda i: (i, 0)),
                  pl.BlockSpec((1, d), lambda i: (0, 0)),
                  pl.BlockSpec((d, n), lambda i: (0, 0))],
        out_specs=[tiles, tiles, pl.BlockSpec((tm, n), lambda i: (i, 0))],
        out_shape=[jax.ShapeDtypeStruct((m * SUBLANES, LANES), F32)] * 2 + [jax.ShapeDtypeStruct((m, n), BF16)],
        compiler_params=_params("parallel"),
        name="kv_proj",
    )(x, g.reshape(1, d), w_bf16)


def _rms_kernel(x_ref, g_ref, o_ref):
    x = x_ref[...]
    ms = jnp.mean(x * x, axis=-1, keepdims=True)
    o_ref[...] = x * lax.rsqrt(ms + RMS_EPS) * g_ref[...]


def _rms(x, g, tm):
    m, d = x.shape
    return pl.pallas_call(
        _rms_kernel,
        grid=(m // tm,),
        in_specs=[pl.BlockSpec((tm, d), lambda i: (i, 0)), pl.BlockSpec((1, d), lambda i: (0, 0))],
        out_specs=pl.BlockSpec((tm, d), lambda i: (i, 0)),
        out_shape=jax.ShapeDtypeStruct((m, d), F32),
        compiler_params=_params("parallel"),
        name="rms_final",
    )(x, g.reshape(1, d))


def _win_prompt_kernel(q_ref, kp_ref, kc_ref, vp_ref, vc_ref, o_ref, lse_ref, *, slopes, blocks_per_seq):
    first = (pl.program_id(0) % blocks_per_seq) == 0
    blk = WIN_BLOCK
    qi = lax.broadcasted_iota(jnp.int32, (blk, 2 * blk), 0)
    kj = lax.broadcasted_iota(jnp.int32, (blk, 2 * blk), 1)
    rel = qi + blk - kj
    valid = (rel >= 0) & (rel <= blk) & ((kj >= blk) | jnp.logical_not(first))
    relf = rel.astype(F32)
    lane = lax.broadcasted_iota(jnp.int32, (blk, LANES), 1)
    lse_all = jnp.zeros((blk, LANES), F32)
    for h in range(WIN_HEADS):
        hs = slice(h * HEAD_DIM, (h + 1) * HEAD_DIM)
        q = q_ref[:, hs].astype(BF16)
        k = jnp.concatenate([kp_ref[:, hs], kc_ref[:, hs]], axis=0).astype(BF16)
        v = jnp.concatenate([vp_ref[:, hs], vc_ref[:, hs]], axis=0).astype(BF16)
        s = _dot_t(q, k) * SCALE
        s = jnp.where(valid, s - slopes[h] * relf, -jnp.inf)
        m = jnp.max(s, axis=-1, keepdims=True)
        p = jnp.exp(s - m)
        den = jnp.sum(p, axis=-1, keepdims=True)
        o = jnp.dot(p.astype(BF16), v, preferred_element_type=F32)
        o_ref[:, hs] = o / den
        lse_all = jnp.where(lane == h, m + jnp.log(den), lse_all)
    lse_ref[...] = lse_all


def _win_prompt_strided_kernel(slope_ref, q_ref, kp_ref, kc_ref, vp_ref, vc_ref, o_ref, lse_ref, *,
                               g, chunks_per_seq):
    dil = WIN_GROUPS[g][1]
    blk = WIN_BLOCK
    h = pl.program_id(1)
    first = (pl.program_id(0) % chunks_per_seq) == 0
    slope = slope_ref[g * WIN_HEADS + h] * dil
    qi = lax.broadcasted_iota(jnp.int32, (blk, 2 * blk), 0)
    kj = lax.broadcasted_iota(jnp.int32, (blk, 2 * blk), 1)
    rel = qi + blk - kj
    valid = (rel >= 0) & (rel <= blk) & ((kj >= blk) | jnp.logical_not(first))
    bias = slope * rel.astype(F32)
    lane = lax.broadcasted_iota(jnp.int32, (blk, LANES), 1)

    @pl.when(h == 0)
    def _():
        lse_ref[...] = jnp.zeros_like(lse_ref)

    def body(r, carry):
        sub = pl.ds(r, blk, stride=dil)
        q = q_ref[sub, :].astype(BF16)
        k = jnp.concatenate([kp_ref[sub, :], kc_ref[sub, :]], axis=0).astype(BF16)
        v = jnp.concatenate([vp_ref[sub, :], vc_ref[sub, :]], axis=0).astype(BF16)
        s = _dot_t(q, k) * SCALE
        s = jnp.where(valid, s - bias, -jnp.inf)
        m = jnp.max(s, axis=-1, keepdims=True)
        p = jnp.exp(s - m)
        den = jnp.sum(p, axis=-1, keepdims=True)
        o_ref[sub, :] = jnp.dot(p.astype(BF16), v, preferred_element_type=F32) / den
        lse_ref[sub, :] = jnp.where(lane == h, m + jnp.log(den), lse_ref[sub, :])
        return carry

    lax.fori_loop(0, dil, body, 0)


def _win_prompt_strided(p_a, g, seq):
    n, width = p_a.shape
    _, dil = WIN_GROUPS[g]
    rows = WIN_BLOCK * dil
    assert seq % rows == 0 and n % seq == 0
    gw = WIN_HEADS * HEAD_DIM
    k_col, v_col = N_WIN_GROUPS * WIN_HEADS, 2 * N_WIN_GROUPS * WIN_HEADS
    slopes = jnp.asarray([_alibi_slope(gg, hh) for gg in range(N_WIN_GROUPS) for hh in range(WIN_HEADS)], F32)

    def spec(col0, prev):
        if prev:
            return pl.BlockSpec((rows, HEAD_DIM), lambda c, h: (jnp.maximum(c - 1, 0), col0 + g * WIN_HEADS + h))
        return pl.BlockSpec((rows, HEAD_DIM), lambda c, h: (c, col0 + g * WIN_HEADS + h))

    return pl.pallas_call(
        functools.partial(_win_prompt_strided_kernel, g=g, chunks_per_seq=seq // rows),
        grid=(n // rows, WIN_HEADS),
        in_specs=[pl.BlockSpec(memory_space=pltpu.SMEM),
                  spec(0, False), spec(k_col, True), spec(k_col, False), spec(v_col, True), spec(v_col, False)],
        out_specs=[pl.BlockSpec((rows, HEAD_DIM), lambda c, h: (c, h)),
                   pl.BlockSpec((rows, LANES), lambda c, h: (c, 0))],
        out_shape=[jax.ShapeDtypeStruct((n, gw), F32), jax.ShapeDtypeStruct((n, LANES), F32)],
        compiler_params=_params("parallel", "arbitrary"),
        name="win_prompt_g%d" % g,
    )(slopes, p_a, p_a, p_a, p_a, p_a)


def _win_prompt(p_a, g, seq):
    n, width = p_a.shape
    _, dil = WIN_GROUPS[g]
    if dil > 1:
        return _win_prompt_strided(p_a, g, seq)
    sub = seq // dil
    assert sub % WIN_BLOCK == 0 and n % seq == 0
    gw = WIN_HEADS * HEAD_DIM
    cb = width // gw
    nq = N_WIN_GROUPS
    pa_r = p_a.reshape(n // dil, dil * width)
    nb = n // dil // WIN_BLOCK
    blocks_per_seq = sub // WIN_BLOCK
    slopes = tuple(_alibi_slope(g, h) * dil for h in range(WIN_HEADS))

    def spec(col, prev):
        if prev:
            return pl.BlockSpec((WIN_BLOCK, gw), lambda ib, r: (jnp.maximum(ib - 1, 0), r * cb + col))
        return pl.BlockSpec((WIN_BLOCK, gw), lambda ib, r: (ib, r * cb + col))

    o, lse = pl.pallas_call(
        functools.partial(_win_prompt_kernel, slopes=slopes, blocks_per_seq=blocks_per_seq),
        grid=(nb, dil),
        in_specs=[spec(g, False), spec(nq + g, True), spec(nq + g, False),
                  spec(2 * nq + g, True), spec(2 * nq + g, False)],
        out_specs=[pl.BlockSpec((WIN_BLOCK, gw), lambda ib, r: (ib, r)),
                   pl.BlockSpec((WIN_BLOCK, LANES), lambda ib, r: (ib, r))],
        out_shape=[jax.ShapeDtypeStruct((n // dil, dil * gw), F32),
                   jax.ShapeDtypeStruct((n // dil, dil * LANES), F32)],
        compiler_params=_params("parallel", "parallel"),
        name="win_prompt_g%d" % g,
    )(pa_r, pa_r, pa_r, pa_r, pa_r)
    return o.reshape(n, gw), lse.reshape(n, LANES)


def _win_step_kernel(q_ref, kn_ref, vn_ref, kc_ref, vc_ref, o_ref, l_ref, *, g):
    bb = q_ref.shape[0]
    dil = WIN_GROUPS[g][1]
    lane = lax.broadcasted_iota(jnp.int32, (bb, LANES), 1)
    steps = (WIN_BLOCK - lax.broadcasted_iota(jnp.int32, (1, WIN_BLOCK, 1), 1)).astype(F32)
    lse_all = jnp.zeros((bb, LANES), F32)
    for h in range(WIN_HEADS):
        hs = slice(h * HEAD_DIM, (h + 1) * HEAD_DIM)
        q = q_ref[:, hs]
        kn = kn_ref[:, hs]
        vn = vn_ref[:, hs]
        kc = kc_ref[:, :, h, :]
        vc = vc_ref[:, :, h, :]
        s_c = jnp.sum(kc * q[:, None, :], axis=-1, keepdims=True) * SCALE
        s_c = s_c - (_alibi_slope(g, h) * dil) * steps
        s_n = jnp.sum(kn * q, axis=-1, keepdims=True) * SCALE
        m = jnp.maximum(jnp.max(s_c, axis=1), s_n)
        p_c = jnp.exp(s_c - m[:, None, :])
        p_n = jnp.exp(s_n - m)
        den = jnp.sum(p_c, axis=1) + p_n
        p_c = p_c / den[:, None, :]
        p_n = p_n / den
        o_ref[:, hs] = jnp.sum(p_c * vc, axis=1) + p_n * vn
        lse_all = jnp.where(lane == h, m + jnp.log(den), lse_all)
    l_ref[...] = lse_all


def _win_step(pa_s, g, k_cache, v_cache, layer):
    nseq, width = pa_s.shape
    gw = WIN_HEADS * HEAD_DIM
    win, dil = WIN_GROUPS[g]
    assert k_cache.shape[2] == win, "cached window rows must cover the whole window"
    bb = 8
    col = lambda c: pl.BlockSpec((bb, gw), lambda i: (i, c))
    cache_spec = pl.BlockSpec((None, bb, WIN_BLOCK, None, WIN_HEADS, HEAD_DIM), lambda i: (layer, i, 0, 0, 0, 0))
    split = lambda c: c.reshape(c.shape[0], nseq, WIN_BLOCK, dil, WIN_HEADS, HEAD_DIM)
    return pl.pallas_call(
        functools.partial(_win_step_kernel, g=g),
        grid=(nseq // bb,),
        in_specs=[col(g), col(N_WIN_GROUPS + g), col(2 * N_WIN_GROUPS + g), cache_spec, cache_spec],
        out_specs=[pl.BlockSpec((bb, gw), lambda i: (i, 0)), pl.BlockSpec((bb, LANES), lambda i: (i, 0))],
        out_shape=[jax.ShapeDtypeStruct((nseq, gw), F32), jax.ShapeDtypeStruct((nseq, LANES), F32)],
        compiler_params=_params("parallel"),
        name="win_step_g%d" % g,
    )(pa_s, pa_s, pa_s, split(k_cache), split(v_cache))


def _mem_prompt_kernel(q_ref, mk_ref, mv_ref, o_ref):
    for h in range(MEM_HEADS):
        hs = slice(h * HEAD_DIM, (h + 1) * HEAD_DIM)
        q = q_ref[:, hs].astype(BF16)
        s = _dot_t(q, mk_ref[:, hs].astype(BF16)) * SCALE
        m = jnp.max(s, axis=-1, keepdims=True)
        p = jnp.exp(s - m)
        den = jnp.sum(p, axis=-1, keepdims=True)
        o = jnp.dot(p.astype(BF16), mv_ref[:, hs].astype(BF16), preferred_element_type=F32)
        o_ref[:, hs] = o / den


def _mem_prompt(p, q_col_block, memkv, seq, tm):
    n = p.shape[0]
    mw = MEM_HEADS * HEAD_DIM
    mtok = memkv.shape[0] // (n // seq)
    per_seq = seq // tm
    return pl.pallas_call(
        _mem_prompt_kernel,
        grid=(n // tm,),
        in_specs=[pl.BlockSpec((tm, mw), lambda i: (i, q_col_block)),
                  pl.BlockSpec((mtok, mw), lambda i: (i // per_seq, 0)),
                  pl.BlockSpec((mtok, mw), lambda i: (i // per_seq, 1))],
        out_specs=pl.BlockSpec((tm, mw), lambda i: (i, 0)),
        out_shape=jax.ShapeDtypeStruct((n, mw), F32),
        compiler_params=_params("parallel"),
        name="mem_prompt",
    )(p, memkv, memkv)


def _mem_step_kernel(q_ref, mk_ref, mv_ref, o_ref):
    for h in range(MEM_HEADS):
        hs = slice(h * HEAD_DIM, (h + 1) * HEAD_DIM)
        q = q_ref[:, hs]
        mk = mk_ref[:, :, h, :]
        mv = mv_ref[:, :, h, :]
        s = jnp.sum(mk * q[:, None, :], axis=-1, keepdims=True) * SCALE
        m = jnp.max(s, axis=1, keepdims=True)
        p = jnp.exp(s - m)
        den = jnp.sum(p, axis=1, keepdims=True)
        o_ref[:, hs] = jnp.sum((p / den) * mv, axis=1)


def _mem_step(p, q_col_block, mk, mv, layer):
    nseq = p.shape[0]
    mw = MEM_HEADS * HEAD_DIM
    mtok = mk.shape[2]
    bb = 8
    cache_spec = pl.BlockSpec((None, bb, mtok, MEM_HEADS, HEAD_DIM), lambda i: (layer, i, 0, 0, 0))
    return pl.pallas_call(
        _mem_step_kernel,
        grid=(nseq // bb,),
        in_specs=[pl.BlockSpec((bb, mw), lambda i: (i, q_col_block)), cache_spec, cache_spec],
        out_specs=pl.BlockSpec((bb, mw), lambda i: (i, 0)),
        out_shape=jax.ShapeDtypeStruct((nseq, mw), F32),
        compiler_params=_params("parallel"),
        name="mem_step",
    )(p, mk, mv)


def _out_a_kernel(o0_ref, o1_ref, o2_ref, l0_ref, l1_ref, l2_ref, om_ref, h_ref, w_ref, out_ref):
    l0, l1, l2 = l0_ref[...], l1_ref[...], l2_ref[...]
    mx = jnp.maximum(jnp.maximum(l0, l1), l2)
    e0, e1, e2 = jnp.exp(l0 - mx), jnp.exp(l1 - mx), jnp.exp(l2 - mx)
    den = e0 + e1 + e2
    w0, w1, w2 = e0 / den, e1 / den, e2 / den
    parts = []
    for h in range(WIN_HEADS):
        hs = slice(h * HEAD_DIM, (h + 1) * HEAD_DIM)
        parts.append(w0[:, h:h + 1] * o0_ref[:, hs] + w1[:, h:h + 1] * o1_ref[:, hs] + w2[:, h:h + 1] * o2_ref[:, hs])
    parts.append(om_ref[...])
    cat = jnp.concatenate(parts, axis=1).astype(BF16)
    out_ref[...] = h_ref[...] + jnp.dot(cat, w_ref[...], preferred_element_type=F32)


def _out_a(outs, lses, om, h, w_bf16, tm):
    n, d = h.shape
    gw = WIN_HEADS * HEAD_DIM
    row = lambda c: pl.BlockSpec((tm, c), lambda i: (i, 0))
    return pl.pallas_call(
        _out_a_kernel,
        grid=(n // tm,),
        in_specs=[row(gw)] * 3 + [row(LANES)] * 3 + [row(om.shape[1]), row(d),
                                                    pl.BlockSpec(w_bf16.shape, lambda i: (0, 0))],
        out_specs=row(d),
        out_shape=jax.ShapeDtypeStruct((n, d), F32),
        compiler_params=_params("parallel"),
        name="out_a",
    )(*outs, *lses, om, h, w_bf16)


def _out_b_kernel(a_ref, b_ref, h_ref, w_ref, out_ref):
    ka = a_ref.shape[1]
    acc = jnp.dot(a_ref[...].astype(BF16), w_ref[:ka, :], preferred_element_type=F32)
    acc = acc + jnp.dot(b_ref[...].astype(BF16), w_ref[ka:, :], preferred_element_type=F32)
    out_ref[...] = h_ref[...] + acc


def _out_b(a, b, h, w_bf16, tm):
    n, d = h.shape
    row = lambda c: pl.BlockSpec((tm, c), lambda i: (i, 0))
    return pl.pallas_call(
        _out_b_kernel,
        grid=(n // tm,),
        in_specs=[row(a.shape[1]), row(b.shape[1]), row(d), pl.BlockSpec(w_bf16.shape, lambda i: (0, 0))],
        out_specs=row(d),
        out_shape=jax.ShapeDtypeStruct((n, d), F32),
        compiler_params=_params("parallel"),
        name="out_b",
    )(a, b, h, w_bf16)


def _sb_prompt_kernel(q_ref, k_ref, v_ref, b_ref, o_ref):
    t = SB_TILE
    i = pl.program_id(2)
    heads = q_ref.shape[1] // HEAD_DIM
    head0 = pl.program_id(1) * heads
    row = lax.broadcasted_iota(jnp.int32, (t, t), 0)
    col = lax.broadcasted_iota(jnp.int32, (t, t), 1)
    later = (row > col).astype(BF16)
    strict = col < row
    cols = [slice(h * HEAD_DIM, (h + 1) * HEAD_DIM) for h in range(heads)]
    qs = [(q_ref[:, c] * SCALE).astype(BF16) for c in cols]
    biases = [b_ref[0, head0 + h] for h in range(heads)]

    def run(tiles, carry):
        units = [(h, pl.multiple_of(j * t, t), mask) for h in range(heads) for j, mask in tiles]
        zs = [_dot_t(qs[h], k_ref[pl.ds(s, t), cols[h]]) + biases[h] for h, s, _ in units]
        sps = []
        for (_, _, mask), z in zip(units, zs):
            sp = _softplus(z)
            sps.append(sp if mask is None else jnp.where(mask, sp, 0.0))
        afters = []
        for sp in sps:
            hi = sp.astype(BF16)
            lo = (sp - hi.astype(F32)).astype(BF16)
            afters.append(jnp.dot(hi, later, preferred_element_type=F32)
                          + jnp.dot(lo, later, preferred_element_type=F32))
        carry = list(carry)
        for (h, s, mask), z, sp, after in zip(units, zs, sps, afters):
            acc, cm = carry[h]
            a = jnp.exp(z - sp - after - cm)
            if mask is not None:
                a = jnp.where(mask, a, 0.0)
            acc = acc + jnp.dot(a.astype(BF16), v_ref[pl.ds(s, t), cols[h]], preferred_element_type=F32)
            carry[h] = (acc, cm + jnp.sum(sp, axis=-1, keepdims=True))
        return tuple(carry)

    has_partner = (i % 2) == 1
    partner_mask = row < jnp.where(has_partner, t, 0)
    init = tuple((jnp.zeros((t, HEAD_DIM), F32), jnp.zeros((t, 1), F32)) for _ in range(heads))
    carry = run([(i, strict), (jnp.maximum(i - 1, 0), partner_mask)], init)
    rest = i - has_partner.astype(jnp.int32)

    def pair(p, c):
        j = rest - 1 - 2 * p
        return run([(j, None), (j - 1, None)], c)

    carry = lax.fori_loop(0, rest // 2, pair, carry)
    for h in range(heads):
        o_ref[:, cols[h]] = carry[h][0]


def _sb_prompt(p_b, kv_bf16, bias, seq, heads_per_step=2):
    n = kv_bf16.shape[0]
    t = SB_TILE
    nq = seq // t
    hw = heads_per_step * HEAD_DIM
    groups = SB_HEADS // heads_per_step
    return pl.pallas_call(
        _sb_prompt_kernel,
        grid=(n // seq, groups, nq),
        in_specs=[pl.BlockSpec((t, hw), lambda b, h, i: (b * nq + i, h)),
                  pl.BlockSpec((seq, hw), lambda b, h, i: (b, h)),
                  pl.BlockSpec((seq, hw), lambda b, h, i: (b, groups + h)),
                  pl.BlockSpec(memory_space=pltpu.SMEM)],
        out_specs=pl.BlockSpec((t, hw), lambda b, h, i: (b * nq + i, h)),
        out_shape=jax.ShapeDtypeStruct((n, SB_HEADS * HEAD_DIM), F32),
        compiler_params=_params("parallel", "parallel", "parallel"),
        name="sb_prompt",
    )(p_b, kv_bf16, kv_bf16, bias.reshape(1, SB_HEADS))


def _sb_step_kernel(pt_ref, q_ref, b_ref, *refs, pages_per_step):
    del pt_ref
    k_refs = refs[:pages_per_step]
    v_refs = refs[pages_per_step:2 * pages_per_step]
    o_ref = refs[2 * pages_per_step]
    acc_ref, cm_ref = refs[2 * pages_per_step + 1:]
    c = pl.program_id(1)
    page = k_refs[0].shape[0] // SB_HEADS

    def head_rows(ref, h):
        return ref[pl.ds(h, page, stride=SB_HEADS), :]

    @pl.when(c == 0)
    def _():
        acc_ref[...] = jnp.zeros_like(acc_ref)
        cm_ref[...] = jnp.zeros_like(cm_ref)

    q = q_ref[0] * SCALE
    bias = b_ref[...]
    lane = lax.broadcasted_iota(jnp.int32, (page, LANES), 1)
    row = lax.broadcasted_iota(jnp.int32, (page, page), 0)
    col = lax.broadcasted_iota(jnp.int32, (page, page), 1)
    later = (col > row).astype(BF16)
    acc = acc_ref[...]
    cm = cm_ref[...]
    zs = []
    for k_ref in k_refs:
        z = jnp.zeros((page, LANES), F32)
        for h in range(SB_HEADS):
            zh = jnp.sum(head_rows(k_ref, h) * q[:, h * HEAD_DIM:(h + 1) * HEAD_DIM], axis=-1, keepdims=True)
            z = jnp.where(lane == h, zh, z)
        zs.append(z + bias)
    sps = [_softplus(z) for z in zs]
    afters = []
    for sp in sps:
        hi = sp.astype(BF16)
        lo = (sp - hi.astype(F32)).astype(BF16)
        afters.append(jnp.dot(later, hi, preferred_element_type=F32) + jnp.dot(later, lo, preferred_element_type=F32))
    for v_ref, z, sp, after in zip(v_refs, zs, sps, afters):
        a = jnp.exp(z - sp - after - cm)
        parts = []
        for h in range(SB_HEADS):
            av = a[:, h:h + 1] * head_rows(v_ref, h)
            parts.append(jnp.sum(av.reshape(page // SUBLANES, SUBLANES, HEAD_DIM), axis=0))
        acc = acc + jnp.concatenate(parts, axis=1)
        cm = cm + jnp.sum(sp, axis=0, keepdims=True)
    acc_ref[...] = acc
    cm_ref[...] = cm

    @pl.when(c == pl.num_programs(1) - 1)
    def _():
        o_ref[0] = jnp.sum(acc, axis=0, keepdims=True)


def _sb_step(p_b, k_pages, v_pages, page_table, bias, pages_per_step=8):
    nseq = p_b.shape[0]
    page = k_pages.shape[1]
    n_pages = page_table.shape[1]
    w = SB_HEADS * HEAD_DIM
    assert n_pages % pages_per_step == 0
    chunks = n_pages // pages_per_step
    assert k_pages.shape[2:] == (SB_HEADS, HEAD_DIM) and SB_HEADS == SUBLANES
    kp = k_pages.reshape(k_pages.shape[0], page * SB_HEADS, HEAD_DIM)
    vp = v_pages.reshape(v_pages.shape[0], page * SB_HEADS, HEAD_DIM)
    bias_row = jnp.zeros((1, LANES), F32).at[0, :SB_HEADS].set(bias)

    def page_spec(u):
        return pl.BlockSpec((None, page * SB_HEADS, HEAD_DIM),
                            lambda s, c, pt: (pt[s, n_pages - 1 - (c * pages_per_step + u)], 0, 0))

    grid_spec = pltpu.PrefetchScalarGridSpec(
        num_scalar_prefetch=1,
        grid=(nseq, chunks),
        in_specs=[pl.BlockSpec((1, 1, w), lambda s, c, pt: (s, 0, 0)),
                  pl.BlockSpec((1, LANES), lambda s, c, pt: (0, 0))]
                 + [page_spec(u) for u in range(pages_per_step)] * 2,
        out_specs=pl.BlockSpec((1, 1, w), lambda s, c, pt: (s, 0, 0)),
        scratch_shapes=[pltpu.VMEM((SUBLANES, w), F32), pltpu.VMEM((1, LANES), F32)],
    )
    q3 = p_b[:, :w].reshape(nseq, 1, w)
    out = pl.pallas_call(
        functools.partial(_sb_step_kernel, pages_per_step=pages_per_step),
        grid_spec=grid_spec,
        out_shape=jax.ShapeDtypeStruct((nseq, 1, w), F32),
        compiler_params=_params("parallel", "arbitrary"),
        name="sb_step",
    )(page_table, q3, bias_row, *([kp] * pages_per_step), *([vp] * pages_per_step))
    return out.reshape(nseq, w)


def _store_row_tiles(ref, val):
    rows = val.shape[0]
    for s in range(SUBLANES):
        ref[pl.ds(s, rows, stride=SUBLANES), :] = val[:, s * LANES:(s + 1) * LANES]


def _load_row_tiles(ref):
    rows = ref.shape[0] // SUBLANES
    return jnp.concatenate([ref[pl.ds(s, rows, stride=SUBLANES), :] for s in range(SUBLANES)], axis=1)


def _row_tile(ref, r):
    return ref.at[pl.ds(pl.multiple_of(r * SUBLANES, SUBLANES), SUBLANES), :]


def _norm_router_kernel(xp_ref, xs_ref, g_ref, wr_ref, br_ref, xn_ref, lg_ref, *, prompt_tiles):
    x = jnp.where(pl.program_id(0) < prompt_tiles, xp_ref[...], xs_ref[...])
    ms = jnp.mean(x * x, axis=-1, keepdims=True)
    xn = x * lax.rsqrt(ms + RMS_EPS) * g_ref[...]
    _store_row_tiles(xn_ref, xn)
    lg_ref[...] = jnp.dot(xn, wr_ref[...], preferred_element_type=F32, precision=lax.Precision.HIGHEST) + br_ref[...]


def _norm_router(h_p, h_s, g, wr, br, tm):
    n_p, d = h_p.shape
    n_s = h_s.shape[0]
    assert n_p % tm == 0 and n_s % tm == 0 and d == SUBLANES * LANES
    tp, ts = n_p // tm, n_s // tm
    return pl.pallas_call(
        functools.partial(_norm_router_kernel, prompt_tiles=tp),
        grid=(tp + ts,),
        in_specs=[pl.BlockSpec((tm, d), lambda i: (jnp.minimum(i, tp - 1), 0)),
                  pl.BlockSpec((tm, d), lambda i: (jnp.maximum(i - tp, 0), 0)),
                  pl.BlockSpec((1, d), lambda i: (0, 0)),
                  pl.BlockSpec((d, LANES), lambda i: (0, 0)),
                  pl.BlockSpec((1, LANES), lambda i: (0, 0))],
        out_specs=[pl.BlockSpec((tm * SUBLANES, LANES), lambda i: (i, 0)),
                   pl.BlockSpec((tm, LANES), lambda i: (i, 0))],
        out_shape=[jax.ShapeDtypeStruct(((n_p + n_s) * SUBLANES, LANES), F32),
                   jax.ShapeDtypeStruct((n_p + n_s, LANES), F32)],
        compiler_params=_params("parallel"),
        name="norm_router",
    )(h_p, h_s, g.reshape(1, d), wr, br)


def _expert_kernel(be_ref, tok_ref, base_ref, nu_ref, xn_hbm, wg_ref, wu_ref, wd_ref, yr_ref,
                   xbuf, sem, wg_s, wu_s, wd_s):
    i = pl.program_id(0)
    n_used = nu_ref[0]
    rows = MOE_ROWS

    def row_copy(base, slot, r):
        tok = tok_ref[base + r]
        return pltpu.make_async_copy(_row_tile(xn_hbm, tok), _row_tile(xbuf.at[slot], r), sem.at[slot])

    def start_gather(blk, slot):
        base = base_ref[blk]

        def body(r, carry):
            row_copy(base, slot, r).start()
            return carry
        lax.fori_loop(0, rows, body, 0, unroll=GATHER_UNROLL)

    def wait_gather(blk, slot):
        base = base_ref[blk]

        def body(r, carry):
            row_copy(base, slot, r).wait()
            return carry
        lax.fori_loop(0, rows, body, 0, unroll=GATHER_UNROLL)

    @pl.when(jnp.logical_and(i == 0, n_used > 0))
    def _():
        start_gather(0, 0)

    @pl.when(i + 1 < n_used)
    def _():
        start_gather(i + 1, (i + 1) % 2)

    @pl.when(i >= n_used)
    def _():
        yr_ref[...] = jnp.zeros_like(yr_ref)

    @pl.when(i < n_used)
    def _():
        slot = i % 2
        changed = jnp.logical_or(i == 0, be_ref[i] != be_ref[jnp.maximum(i - 1, 0)])

        @pl.when(changed)
        def _():
            wg_s[...] = wg_ref[...].astype(BF16)
            wu_s[...] = wu_ref[...].astype(BF16)
            wd_s[...] = wd_ref[...].astype(BF16)

        wait_gather(i, slot)
        x = _load_row_tiles(xbuf.at[slot]).astype(BF16)
        gate = jnp.dot(x, wg_s[...], preferred_element_type=F32)
        up = jnp.dot(x, wu_s[...], preferred_element_type=F32)
        mid = (gate * jax.nn.sigmoid(gate) * up).astype(BF16)
        y = jnp.dot(mid, wd_s[...], preferred_element_type=F32)
        _store_row_tiles(yr_ref, y)


def _experts(xn_all, blk_exp, sorted_tok, blk_base, n_used, wg, wu, wd, layer):
    n_blk = blk_exp.shape[0]
    n_rows = n_blk * MOE_ROWS
    d, de = wg.shape[2:]
    grid_spec = pltpu.PrefetchScalarGridSpec(
        num_scalar_prefetch=4,
        grid=(n_blk,),
        in_specs=[pl.BlockSpec(memory_space=pl.ANY),
                  pl.BlockSpec((None, None, d, de), lambda i, be, *_: (layer, be[i], 0, 0)),
                  pl.BlockSpec((None, None, d, de), lambda i, be, *_: (layer, be[i], 0, 0)),
                  pl.BlockSpec((None, None, de, d), lambda i, be, *_: (layer, be[i], 0, 0))],
        out_specs=pl.BlockSpec((MOE_ROWS * SUBLANES, LANES), lambda i, *_: (i, 0)),
        scratch_shapes=[pltpu.VMEM((2, MOE_ROWS * SUBLANES, LANES), F32), pltpu.SemaphoreType.DMA((2,)),
                        pltpu.VMEM((d, de), BF16), pltpu.VMEM((d, de), BF16), pltpu.VMEM((de, d), BF16)],
    )
    return pl.pallas_call(
        _expert_kernel,
        grid_spec=grid_spec,
        out_shape=jax.ShapeDtypeStruct((n_rows * SUBLANES, LANES), F32),
        compiler_params=_params("arbitrary"),
        name="moe_experts",
    )(blk_exp, sorted_tok, blk_base, n_used, xn_all, wg, wu, wd)


def _combine_kernel(p0_ref, p1_ref, yr_hbm, g0_ref, g1_ref, h_ref, out_ref, buf, sem, *, tc):
    i = pl.program_id(0)
    nsteps = pl.num_programs(0)

    def row_copy(step, slot, r, which):
        pos_ref = p1_ref if which else p0_ref
        pos = pos_ref[step * tc + r]
        return pltpu.make_async_copy(_row_tile(yr_hbm, pos), _row_tile(buf.at[slot, which], r), sem.at[slot])

    def start_gather(step, slot):
        def body(r, carry):
            row_copy(step, slot, r, 0).start()
            row_copy(step, slot, r, 1).start()
            return carry
        lax.fori_loop(0, tc, body, 0, unroll=GATHER_UNROLL)

    def wait_gather(step, slot):
        def body(r, carry):
            row_copy(step, slot, r, 0).wait()
            row_copy(step, slot, r, 1).wait()
            return carry
        lax.fori_loop(0, tc, body, 0, unroll=GATHER_UNROLL)

    @pl.when(i == 0)
    def _():
        start_gather(0, 0)

    @pl.when(i + 1 < nsteps)
    def _():
        start_gather(i + 1, (i + 1) % 2)

    slot = i % 2
    wait_gather(i, slot)
    y = g0_ref[...] * _load_row_tiles(buf.at[slot, 0]) + g1_ref[...] * _load_row_tiles(buf.at[slot, 1])
    out_ref[...] = h_ref[...] + y


def _combine(yr, pos, gates, h, tc):
    n, d = h.shape
    assert n % tc == 0
    pos0, pos1 = pos[:, 0], pos[:, 1]
    g0, g1 = gates[:, 0:1], gates[:, 1:2]
    grid_spec = pltpu.PrefetchScalarGridSpec(
        num_scalar_prefetch=2,
        grid=(n // tc,),
        in_specs=[pl.BlockSpec(memory_space=pl.ANY),
                  pl.BlockSpec((tc, 1), lambda i, p0, p1: (i, 0)),
                  pl.BlockSpec((tc, 1), lambda i, p0, p1: (i, 0)),
                  pl.BlockSpec((tc, d), lambda i, p0, p1: (i, 0))],
        out_specs=pl.BlockSpec((tc, d), lambda i, p0, p1: (i, 0)),
        scratch_shapes=[pltpu.VMEM((2, 2, tc * SUBLANES, LANES), F32), pltpu.SemaphoreType.DMA((2,))],
    )
    return pl.pallas_call(
        functools.partial(_combine_kernel, tc=tc),
        grid_spec=grid_spec,
        out_shape=jax.ShapeDtypeStruct((n, d), F32),
        compiler_params=_params("arbitrary"),
        name="moe_combine",
    )(pos0, pos1, yr, g0, g1, h)


def _route(logits):
    n = logits.shape[0]
    lg = logits[:, :MOE_GROUPS]
    top_g = jnp.argmax(lg, axis=-1)
    p_top = jnp.take_along_axis(jax.nn.softmax(lg, axis=-1), top_g[:, None], axis=-1)
    le = logits[:, MOE_GROUPS:MOE_GROUPS + N_EXPERTS].reshape(n, MOE_GROUPS, EXPERTS_PER_GROUP)
    le = jnp.take_along_axis(le, top_g[:, None, None], axis=1)[:, 0]
    vals, idx = lax.top_k(le, 2)
    gates = p_top * jax.nn.softmax(vals, axis=-1)
    eid = top_g[:, None] * EXPERTS_PER_GROUP + idx
    return eid.astype(jnp.int32), gates


def _dispatch(eid):
    n = eid.shape[0]
    n_assign = 2 * n
    e_flat = eid.reshape(n_assign)
    onehot = (e_flat[:, None] == jnp.arange(N_EXPERTS, dtype=jnp.int32)[None, :]).astype(jnp.int32)
    csum = jnp.cumsum(onehot, axis=0)
    counts = csum[-1]
    rank = jnp.sum((csum - onehot) * onehot, axis=1)
    padded = (counts + MOE_ROWS - 1) // MOE_ROWS * MOE_ROWS
    pend = jnp.cumsum(padded)
    pstart = pend - padded
    dest = (pstart[e_flat] + rank).astype(jnp.int32)
    n_blk = (n_assign + N_EXPERTS * (MOE_ROWS - 1) + MOE_ROWS - 1) // MOE_ROWS
    blk_start = jnp.arange(n_blk, dtype=jnp.int32) * MOE_ROWS
    blk_exp = jnp.minimum(jnp.sum(blk_start[:, None] >= pend[None, :], axis=1), N_EXPERTS - 1).astype(jnp.int32)
    n_used = (pend[-1] // MOE_ROWS).astype(jnp.int32).reshape(1)
    order = jnp.argsort(e_flat, stable=True).astype(jnp.int32)
    sorted_tok = jnp.concatenate([order // 2, jnp.zeros((MOE_ROWS,), jnp.int32)])
    start = jnp.cumsum(counts) - counts
    blk_base = jnp.clip(start[blk_exp] + blk_start - pstart[blk_exp], 0, n_assign).astype(jnp.int32)
    return sorted_tok, blk_exp, blk_base, n_used, dest.reshape(n, 2)


def _hier_moe(h_p, h_s, g, wr, br, wg, wu, wd, layer):
    n_p, d = h_p.shape
    n_s = h_s.shape[0]
    xn_all, logits = _norm_router(h_p, h_s, g, wr, br, math.gcd(128, n_s))
    eid, gates = _route(logits)
    sorted_tok, blk_exp, blk_base, n_used, pos = _dispatch(eid)
    yr = _experts(xn_all, blk_exp, sorted_tok, blk_base, n_used, wg, wu, wd, layer)
    out_p = _combine(yr, pos[:n_p], gates[:n_p], h_p, 128)
    out_s = _combine(yr, pos[n_p:], gates[n_p:], h_s, min(128, n_s))
    return out_p, out_s


def kernel(x_prompt, x_sample, mem_prompt, cache_win_k0, cache_win_v0, cache_win_k1, cache_win_v1, cache_win_k2, cache_win_v2, cache_mem_k, cache_mem_v, cache_k_pages, cache_v_pages, page_table, g_mix, g_mem, g_ffn, g_kv, g_final, w_in_a, w_out_a, w_in_b, w_out_b, b_sb, w_kv, w_mem_k, w_mem_v, w_router_group, b_router_group, w_router_expert, b_router_expert, w_exp_gate, w_exp_up, w_exp_down):
    bp, seq, d = x_prompt.shape
    bs, t_new, _ = x_sample.shape
    assert t_new == 1, "one new token per sample sequence"
    depth = g_mix.shape[0]
    n_a = w_in_a.shape[0]
    n_p = bp * seq
    tm_p = 512
    tm_s = bs
    mem_tok = mem_prompt.shape[1]
    gw = WIN_HEADS * HEAD_DIM
    win_k_cache = (cache_win_k0, cache_win_k1, cache_win_k2)
    win_v_cache = (cache_win_v0, cache_win_v1, cache_win_v2)

    h = x_prompt.reshape(n_p, d)
    hs = x_sample.reshape(bs, d)
    mem2 = mem_prompt.reshape(bp * mem_tok, d)

    wkp = [[] for _ in WIN_GROUPS]
    wvp = [[] for _ in WIN_GROUPS]
    wks = [[] for _ in WIN_GROUPS]
    wvs = [[] for _ in WIN_GROUPS]
    mkp_list, mvp_list = [], []

    for l in range(depth):
        w_mem = jnp.concatenate([w_mem_k[l], w_mem_v[l]], axis=1).astype(BF16)
        memkv = _rms_matmul(mem2, g_mem[l], w_mem, mem2.shape[0], w_mem.shape[1])
        mw = MEM_HEADS * HEAD_DIM
        mkp_list.append(memkv[:, :mw].reshape(bp, mem_tok, MEM_HEADS, HEAD_DIM))
        mvp_list.append(memkv[:, mw:].reshape(bp, mem_tok, MEM_HEADS, HEAD_DIM))
        if l < n_a:
            w_in = w_in_a[l].astype(BF16)
            pa = _rms_matmul(h, g_mix[l], w_in, tm_p, 2560)
            pa_s = _rms_matmul(hs, g_mix[l], w_in, tm_s, 2560)
            outs, lses = [], []
            for g, (win, dil) in enumerate(WIN_GROUPS):
                o, lse = _win_prompt(pa, g, seq)
                outs.append(o)
                lses.append(lse)
                keep = min(win, seq)
                ka = pa[:, (N_WIN_GROUPS + g) * gw:(N_WIN_GROUPS + g + 1) * gw].reshape(bp, seq, WIN_HEADS, HEAD_DIM)
                va = pa[:, (2 * N_WIN_GROUPS + g) * gw:(2 * N_WIN_GROUPS + g + 1) * gw].reshape(bp, seq, WIN_HEADS, HEAD_DIM)
                wkp[g].append(ka[:, -keep:])
                wvp[g].append(va[:, -keep:])
                kn = pa_s[:, (N_WIN_GROUPS + g) * gw:(N_WIN_GROUPS + g + 1) * gw].reshape(bs, 1, WIN_HEADS, HEAD_DIM)
                vn = pa_s[:, (2 * N_WIN_GROUPS + g) * gw:(2 * N_WIN_GROUPS + g + 1) * gw].reshape(bs, 1, WIN_HEADS, HEAD_DIM)
                keep_s = min(win, win_k_cache[g].shape[2] + 1)
                wks[g].append(jnp.concatenate([win_k_cache[g][l], kn], axis=1)[:, -keep_s:])
                wvs[g].append(jnp.concatenate([win_v_cache[g][l], vn], axis=1)[:, -keep_s:])
            outs_s, lses_s = zip(*[_win_step(pa_s, g, win_k_cache[g], win_v_cache[g], l)
                                   for g in range(N_WIN_GROUPS)])
            q_mem_block = 3 * N_WIN_GROUPS
            om = _mem_prompt(pa, q_mem_block, memkv, seq, tm_p)
            om_s = _mem_step(pa_s, q_mem_block, cache_mem_k, cache_mem_v, l)
            w_out = w_out_a[l].astype(BF16)
            h = _out_a(outs, lses, om, h, w_out, tm_p)
            hs = _out_a(outs_s, lses_s, om_s, hs, w_out, tm_s)
        else:
            lb = l - n_a
            if l == n_a:
                wkv = w_kv.astype(BF16)
                k_p, v_p, kv_p16 = _kv_proj(h, g_kv, wkv, tm_p)
                k_s, v_s, _ = _kv_proj(hs, g_kv, wkv, tm_s)
            w_in = w_in_b[lb].astype(BF16)
            pb = _rms_matmul(h, g_mix[l], w_in, tm_p, w_in.shape[1])
            pb_s = _rms_matmul(hs, g_mix[l], w_in, tm_s, w_in.shape[1])
            osb = _sb_prompt(pb, kv_p16, b_sb[lb], seq)
            osb_s = _sb_step(pb_s, cache_k_pages, cache_v_pages, page_table, b_sb[lb])
            q_mem_block = SB_HEADS * HEAD_DIM // (MEM_HEADS * HEAD_DIM)
            om = _mem_prompt(pb, q_mem_block, memkv, seq, tm_p)
            om_s = _mem_step(pb_s, q_mem_block, cache_mem_k, cache_mem_v, l)
            w_out = w_out_b[lb].astype(BF16)
            h = _out_b(osb, om, h, w_out, tm_p)
            hs = _out_b(osb_s, om_s, hs, w_out, tm_s)
        wr = jnp.zeros((d, LANES), F32).at[:, :MOE_GROUPS].set(w_router_group[l])
        wr = wr.at[:, MOE_GROUPS:MOE_GROUPS + N_EXPERTS].set(w_router_expert[l])
        br = jnp.zeros((1, LANES), F32).at[0, :MOE_GROUPS].set(b_router_group[l])
        br = br.at[0, MOE_GROUPS:MOE_GROUPS + N_EXPERTS].set(b_router_expert[l])
        h, hs = _hier_moe(h, hs, g_ffn[l], wr, br, w_exp_gate, w_exp_up, w_exp_down, l)

    y_prompt = _rms(h, g_final, tm_p).reshape(bp, seq, d)
    y_sample = _rms(hs, g_final, tm_s).reshape(bs, 1, d)
    stack = lambda zs: jnp.stack(zs, axis=0)
    kv_k_p = k_p.reshape(bp, seq, SB_HEADS, HEAD_DIM)
    kv_v_p = v_p.reshape(bp, seq, SB_HEADS, HEAD_DIM)
    kv_k_s = k_s.reshape(bs, 1, SB_HEADS, HEAD_DIM)
    kv_v_s = v_s.reshape(bs, 1, SB_HEADS, HEAD_DIM)
    return (y_prompt, y_sample,
            stack(wkp[0]), stack(wvp[0]), stack(wkp[1]), stack(wvp[1]), stack(wkp[2]), stack(wvp[2]),
            stack(wks[0]), stack(wvs[0]), stack(wks[1]), stack(wvs[1]), stack(wks[2]), stack(wvs[2]),
            stack(mkp_list), stack(mvp_list),
            kv_k_p, kv_v_p, kv_k_s, kv_v_s)
```

```python
import functools
import math

import jax
import jax.numpy as jnp
from jax import lax
from jax.experimental import pallas as pl
from jax.experimental.pallas import tpu as pltpu

HEAD_DIM = 128
LANES = 128
SUBLANES = 8
GATHER_UNROLL = 8
RMS_EPS = 1e-6
WIN_GROUPS = ((128, 1), (512, 4), (2048, 16))
N_WIN_GROUPS = 3
WIN_HEADS = 4
WIN_BLOCK = 128
ALIBI_MAX_BIAS = 8.0
SB_HEADS = 8
MEM_HEADS = 4
MOE_GROUPS = 4
EXPERTS_PER_GROUP = 8
N_EXPERTS = MOE_GROUPS * EXPERTS_PER_GROUP
MOE_ROWS = 256
SB_TILE = 256
SCALE = HEAD_DIM ** -0.5
VMEM_LIMIT = 56 * 1024 * 1024

F32 = jnp.float32
BF16 = jnp.bfloat16


def _alibi_slope(g, h):
    n = N_WIN_GROUPS * WIN_HEADS
    return 2.0 ** (-ALIBI_MAX_BIAS * (g * WIN_HEADS + h + 1) / n)


def _params(*sem):
    return pltpu.CompilerParams(dimension_semantics=sem, vmem_limit_bytes=VMEM_LIMIT)


def _dot_t(a, b):
    return lax.dot_general(a, b, (((1,), (1,)), ((), ())), preferred_element_type=F32)


def _softplus(z):
    return jnp.maximum(z, 0.0) + jnp.log(1.0 + jnp.exp(-jnp.abs(z)))


def _rms_matmul_kernel(x_ref, g_ref, w_ref, o_ref, *maybe_bf16_ref):
    x = x_ref[...]
    ms = jnp.mean(x * x, axis=-1, keepdims=True)
    xn = (x * lax.rsqrt(ms + RMS_EPS) * g_ref[...]).astype(BF16)
    y = jnp.dot(xn, w_ref[...], preferred_element_type=F32)
    o_ref[...] = y
    for r in maybe_bf16_ref:
        r[...] = y.astype(BF16)


def _rms_matmul(x, g, w_bf16, tm, tn, also_bf16=False):
    m, d = x.shape
    n = w_bf16.shape[1]
    assert m % tm == 0 and n % tn == 0
    out_spec = pl.BlockSpec((tm, tn), lambda j, i: (i, j))
    out_specs, out_shape = out_spec, jax.ShapeDtypeStruct((m, n), F32)
    if also_bf16:
        out_specs, out_shape = [out_spec, out_spec], [out_shape, jax.ShapeDtypeStruct((m, n), BF16)]
    return pl.pallas_call(
        _rms_matmul_kernel,
        grid=(n // tn, m // tm),
        in_specs=[pl.BlockSpec((tm, d), lambda j, i: (i, 0)),
                  pl.BlockSpec((1, d), lambda j, i: (0, 0)),
                  pl.BlockSpec((d, tn), lambda j, i: (0, j))],
        out_specs=out_specs,
        out_shape=out_shape,
        compiler_params=_params("parallel", "parallel"),
        name="rms_matmul",
    )(x, g.reshape(1, d), w_bf16)


def _kv_proj_kernel(x_ref, g_ref, w_ref, k_ref, v_ref, kv16_ref):
    x = x_ref[...]
    ms = jnp.mean(x * x, axis=-1, keepdims=True)
    xn = (x * lax.rsqrt(ms + RMS_EPS) * g_ref[...]).astype(BF16)
    y = jnp.dot(xn, w_ref[...], preferred_element_type=F32)
    half = y.shape[1] // 2
    _store_row_tiles(k_ref, y[:, :half])
    _store_row_tiles(v_ref, y[:, half:])
    kv16_ref[...] = y.astype(BF16)


def _kv_proj(x, g, w_bf16, tm):
    m, d = x.shape
    n = w_bf16.shape[1]
    assert m % tm == 0 and n == 2 * SB_HEADS * HEAD_DIM and SB_HEADS == SUBLANES
    tiles = pl.BlockSpec((tm * SUBLANES, LANES), lambda i: (i, 0))
    return pl.pallas_call(
        _kv_proj_kernel,
        grid=(m // tm,),
        in_specs=[pl.BlockSpec((tm, d), lambda i: (i, 0)),
                  pl.BlockSpec((1, d), lambda i: (0, 0)),
                  pl.BlockSpec((d, n), lambda i: (0, 0))],
        out_specs=[tiles, tiles, pl.BlockSpec((tm, n), lambda i: (i, 0))],
        out_shape=[jax.ShapeDtypeStruct((m * SUBLANES, LANES), F32)] * 2 + [jax.ShapeDtypeStruct((m, n), BF16)],
        compiler_params=_params("parallel"),
        name="kv_proj",
    )(x, g.reshape(1, d), w_bf16)


def _rms_kernel(x_ref, g_ref, o_ref):
    x = x_ref[...]
    ms = jnp.mean(x * x, axis=-1, keepdims=True)
    o_ref[...] = x * lax.rsqrt(ms + RMS_EPS) * g_ref[...]


def _rms(x, g, tm):
    m, d = x.shape
    return pl.pallas_call(
        _rms_kernel,
        grid=(m // tm,),
        in_specs=[pl.BlockSpec((tm, d), lambda i: (i, 0)), pl.BlockSpec((1, d), lambda i: (0, 0))],
        out_specs=pl.BlockSpec((tm, d), lambda i: (i, 0)),
        out_shape=jax.ShapeDtypeStruct((m, d), F32),
        compiler_params=_params("parallel"),
        name="rms_final",
    )(x, g.reshape(1, d))


def _win_prompt_kernel(q_ref, kp_ref, kc_ref, vp_ref, vc_ref, o_ref, lse_ref, *, slopes, blocks_per_seq):
    first = (pl.program_id(0) % blocks_per_seq) == 0
    blk = WIN_BLOCK
    qi = lax.broadcasted_iota(jnp.int32, (blk, 2 * blk), 0)
    kj = lax.broadcasted_iota(jnp.int32, (blk, 2 * blk), 1)
    rel = qi + blk - kj
    valid = (rel >= 0) & (rel <= blk) & ((kj >= blk) | jnp.logical_not(first))
    relf = rel.astype(F32)
    lane = lax.broadcasted_iota(jnp.int32, (blk, LANES), 1)
    lse_all = jnp.zeros((blk, LANES), F32)
    for h in range(WIN_HEADS):
        hs = slice(h * HEAD_DIM, (h + 1) * HEAD_DIM)
        q = q_ref[:, hs].astype(BF16)
        k = jnp.concatenate([kp_ref[:, hs], kc_ref[:, hs]], axis=0).astype(BF16)
        v = jnp.concatenate([vp_ref[:, hs], vc_ref[:, hs]], axis=0).astype(BF16)
        s = _dot_t(q, k) * SCALE
        s = jnp.where(valid, s - slopes[h] * relf, -jnp.inf)
        m = jnp.max(s, axis=-1, keepdims=True)
        p = jnp.exp(s - m)
        den = jnp.sum(p, axis=-1, keepdims=True)
        o = jnp.dot(p.astype(BF16), v, preferred_element_type=F32)
        o_ref[:, hs] = o / den
        lse_all = jnp.where(lane == h, m + jnp.log(den), lse_all)
    lse_ref[...] = lse_all


def _win_prompt_strided_kernel(slope_ref, q_ref, kp_ref, kc_ref, vp_ref, vc_ref, o_ref, lse_ref, *,
                               g, chunks_per_seq):
    dil = WIN_GROUPS[g][1]
    blk = WIN_BLOCK
    h = pl.program_id(1)
    first = (pl.program_id(0) % chunks_per_seq) == 0
    slope = slope_ref[g * WIN_HEADS + h] * dil
    qi = lax.broadcasted_iota(jnp.int32, (blk, 2 * blk), 0)
    kj = lax.broadcasted_iota(jnp.int32, (blk, 2 * blk), 1)
    rel = qi + blk - kj
    valid = (rel >= 0) & (rel <= blk) & ((kj >= blk) | jnp.logical_not(first))
    bias = slope * rel.astype(F32)
    lane = lax.broadcasted_iota(jnp.int32, (blk, LANES), 1)

    @pl.when(h == 0)
    def _():
        lse_ref[...] = jnp.zeros_like(lse_ref)

    def body(r, carry):
        sub = pl.ds(r, blk, stride=dil)
        q = q_ref[sub, :].astype(BF16)
        k = jnp.concatenate([kp_ref[sub, :], kc_ref[sub, :]], axis=0).astype(BF16)
        v = jnp.concatenate([vp_ref[sub, :], vc_ref[sub, :]], axis=0).astype(BF16)
        s = _dot_t(q, k) * SCALE
        s = jnp.where(valid, s - bias, -jnp.inf)
        m = jnp.max(s, axis=-1, keepdims=True)
        p = jnp.exp(s - m)
        den = jnp.sum(p, axis=-1, keepdims=True)
        o_ref[sub, :] = jnp.dot(p.astype(BF16), v, preferred_element_type=F32) / den
        lse_ref[sub, :] = jnp.where(lane == h, m + jnp.log(den), lse_ref[sub, :])
        return carry

    lax.fori_loop(0, dil, body, 0)


def _win_prompt_strided(p_a, g, seq):
    n, width = p_a.shape
    _, dil = WIN_GROUPS[g]
    rows = WIN_BLOCK * dil
    assert seq % rows == 0 and n % seq == 0
    gw = WIN_HEADS * HEAD_DIM
    k_col, v_col = N_WIN_GROUPS * WIN_HEADS, 2 * N_WIN_GROUPS * WIN_HEADS
    slopes = jnp.asarray([_alibi_slope(gg, hh) for gg in range(N_WIN_GROUPS) for hh in range(WIN_HEADS)], F32)

    def spec(col0, prev):
        if prev:
            return pl.BlockSpec((rows, HEAD_DIM), lambda c, h: (jnp.maximum(c - 1, 0), col0 + g * WIN_HEADS + h))
        return pl.BlockSpec((rows, HEAD_DIM), lambda c, h: (c, col0 + g * WIN_HEADS + h))

    return pl.pallas_call(
        functools.partial(_win_prompt_strided_kernel, g=g, chunks_per_seq=seq // rows),
        grid=(n // rows, WIN_HEADS),
        in_specs=[pl.BlockSpec(memory_space=pltpu.SMEM),
                  spec(0, False), spec(k_col, True), spec(k_col, False), spec(v_col, True), spec(v_col, False)],
        out_specs=[pl.BlockSpec((rows, HEAD_DIM), lambda c, h: (c, h)),
                   pl.BlockSpec((rows, LANES), lambda c, h: (c, 0))],
        out_shape=[jax.ShapeDtypeStruct((n, gw), F32), jax.ShapeDtypeStruct((n, LANES), F32)],
        compiler_params=_params("parallel", "arbitrary"),
        name="win_prompt_g%d" % g,
    )(slopes, p_a, p_a, p_a, p_a, p_a)


def _win_prompt(p_a, g, seq):
    n, width = p_a.shape
    _, dil = WIN_GROUPS[g]
    if dil > 1:
        return _win_prompt_strided(p_a, g, seq)
    sub = seq // dil
    assert sub % WIN_BLOCK == 0 and n % seq == 0
    gw = WIN_HEADS * HEAD_DIM
    cb = width // gw
    nq = N_WIN_GROUPS
    pa_r = p_a.reshape(n // dil, dil * width)
    nb = n // dil // WIN_BLOCK
    blocks_per_seq = sub // WIN_BLOCK
    slopes = tuple(_alibi_slope(g, h) * dil for h in range(WIN_HEADS))

    def spec(col, prev):
        if prev:
            return pl.BlockSpec((WIN_BLOCK, gw), lambda ib, r: (jnp.maximum(ib - 1, 0), r * cb + col))
        return pl.BlockSpec((WIN_BLOCK, gw), lambda ib, r: (ib, r * cb + col))

    o, lse = pl.pallas_call(
        functools.partial(_win_prompt_kernel, slopes=slopes, blocks_per_seq=blocks_per_seq),
        grid=(nb, dil),
        in_specs=[spec(g, False), spec(nq + g, True), spec(nq + g, False),
                  spec(2 * nq + g, True), spec(2 * nq + g, False)],
        out_specs=[pl.BlockSpec((WIN_BLOCK, gw), lambda ib, r: (ib, r)),
                   pl.BlockSpec((WIN_BLOCK, LANES), lambda ib, r: (ib, r))],
        out_shape=[jax.ShapeDtypeStruct((n // dil, dil * gw), F32),
                   jax.ShapeDtypeStruct((n // dil, dil * LANES), F32)],
        compiler_params=_params("parallel", "parallel"),
        name="win_prompt_g%d" % g,
    )(pa_r, pa_r, pa_r, pa_r, pa_r)
    return o.reshape(n, gw), lse.reshape(n, LANES)


def _win_step_kernel(q_ref, kn_ref, vn_ref, kc_ref, vc_ref, o_ref, l_ref, *, g):
    bb = q_ref.shape[0]
    dil = WIN_GROUPS[g][1]
    lane = lax.broadcasted_iota(jnp.int32, (bb, LANES), 1)
    steps = (WIN_BLOCK - lax.broadcasted_iota(jnp.int32, (1, WIN_BLOCK, 1), 1)).astype(F32)
    lse_all = jnp.zeros((bb, LANES), F32)
    for h in range(WIN_HEADS):
        hs = slice(h * HEAD_DIM, (h + 1) * HEAD_DIM)
        q = q_ref[:, hs]
        kn = kn_ref[:, hs]
        vn = vn_ref[:, hs]
        kc = kc_ref[:, :, h, :]
        vc = vc_ref[:, :, h, :]
        s_c = jnp.sum(kc * q[:, None, :], axis=-1, keepdims=True) * SCALE
        s_c = s_c - (_alibi_slope(g, h) * dil) * steps
        s_n = jnp.sum(kn * q, axis=-1, keepdims=True) * SCALE
        m = jnp.maximum(jnp.max(s_c, axis=1), s_n)
        p_c = jnp.exp(s_c - m[:, None, :])
        p_n = jnp.exp(s_n - m)
        den = jnp.sum(p_c, axis=1) + p_n
        p_c = p_c / den[:, None, :]
        p_n = p_n / den
        o_ref[:, hs] = jnp.sum(p_c * vc, axis=1) + p_n * vn
        lse_all = jnp.where(lane == h, m + jnp.log(den), lse_all)
    l_ref[...] = lse_all


def _win_step(pa_s, g, k_cache, v_cache, layer):
    nseq, width = pa_s.shape
    gw = WIN_HEADS * HEAD_DIM
    win, dil = WIN_GROUPS[g]
    assert k_cache.shape[2] == win, "cached window rows must cover the whole window"
    bb = 8
    col = lambda c: pl.BlockSpec((bb, gw), lambda i: (i, c))
    cache_spec = pl.BlockSpec((None, bb, WIN_BLOCK, None, WIN_HEADS, HEAD_DIM), lambda i: (layer, i, 0, 0, 0, 0))
    split = lambda c: c.reshape(c.shape[0], nseq, WIN_BLOCK, dil, WIN_HEADS, HEAD_DIM)
    return pl.pallas_call(
        functools.partial(_win_step_kernel, g=g),
        grid=(nseq // bb,),
        in_specs=[col(g), col(N_WIN_GROUPS + g), col(2 * N_WIN_GROUPS + g), cache_spec, cache_spec],
        out_specs=[pl.BlockSpec((bb, gw), lambda i: (i, 0)), pl.BlockSpec((bb, LANES), lambda i: (i, 0))],
        out_shape=[jax.ShapeDtypeStruct((nseq, gw), F32), jax.ShapeDtypeStruct((nseq, LANES), F32)],
        compiler_params=_params("parallel"),
        name="win_step_g%d" % g,
    )(pa_s, pa_s, pa_s, split(k_cache), split(v_cache))


def _mem_prompt_kernel(q_ref, mk_ref, mv_ref, o_ref):
    for h in range(MEM_HEADS):
        hs = slice(h * HEAD_DIM, (h + 1) * HEAD_DIM)
        q = q_ref[:, hs].astype(BF16)
        s = _dot_t(q, mk_ref[:, hs].astype(BF16)) * SCALE
        m = jnp.max(s, axis=-1, keepdims=True)
        p = jnp.exp(s - m)
        den = jnp.sum(p, axis=-1, keepdims=True)
        o = jnp.dot(p.astype(BF16), mv_ref[:, hs].astype(BF16), preferred_element_type=F32)
        o_ref[:, hs] = o / den


def _mem_prompt(p, q_col_block, memkv, seq, tm):
    n = p.shape[0]
    mw = MEM_HEADS * HEAD_DIM
    mtok = memkv.shape[0] // (n // seq)
    per_seq = seq // tm
    return pl.pallas_call(
        _mem_prompt_kernel,
        grid=(n // tm,),
        in_specs=[pl.BlockSpec((tm, mw), lambda i: (i, q_col_block)),
                  pl.BlockSpec((mtok, mw), lambda i: (i // per_seq, 0)),
                  pl.BlockSpec((mtok, mw), lambda i: (i // per_seq, 1))],
        out_specs=pl.BlockSpec((tm, mw), lambda i: (i, 0)),
        out_shape=jax.ShapeDtypeStruct((n, mw), F32),
        compiler_params=_params("parallel"),
        name="mem_prompt",
    )(p, memkv, memkv)


def _mem_step_kernel(q_ref, mk_ref, mv_ref, o_ref):
    for h in range(MEM_HEADS):
        hs = slice(h * HEAD_DIM, (h + 1) * HEAD_DIM)
        q = q_ref[:, hs]
        mk = mk_ref[:, :, h, :]
        mv = mv_ref[:, :, h, :]
        s = jnp.sum(mk * q[:, None, :], axis=-1, keepdims=True) * SCALE
        m = jnp.max(s, axis=1, keepdims=True)
        p = jnp.exp(s - m)
        den = jnp.sum(p, axis=1, keepdims=True)
        o_ref[:, hs] = jnp.sum((p / den) * mv, axis=1)


def _mem_step(p, q_col_block, mk, mv, layer):
    nseq = p.shape[0]
    mw = MEM_HEADS * HEAD_DIM
    mtok = mk.shape[2]
    bb = 8
    cache_spec = pl.BlockSpec((None, bb, mtok, MEM_HEADS, HEAD_DIM), lambda i: (layer, i, 0, 0, 0))
    return pl.pallas_call(
        _mem_step_kernel,
        grid=(nseq // bb,),
        in_specs=[pl.BlockSpec((bb, mw), lambda i: (i, q_col_block)), cache_spec, cache_spec],
        out_specs=pl.BlockSpec((bb, mw), lambda i: (i, 0)),
        out_shape=jax.ShapeDtypeStruct((nseq, mw), F32),
        compiler_params=_params("parallel"),
        name="mem_step",
    )(p, mk, mv)


def _out_a_kernel(o0_ref, o1_ref, o2_ref, l0_ref, l1_ref, l2_ref, om_ref, h_ref, w_ref, out_ref):
    l0, l1, l2 = l0_ref[...], l1_ref[...], l2_ref[...]
    mx = jnp.maximum(jnp.maximum(l0, l1), l2)
    e0, e1, e2 = jnp.exp(l0 - mx), jnp.exp(l1 - mx), jnp.exp(l2 - mx)
    den = e0 + e1 + e2
    w0, w1, w2 = e0 / den, e1 / den, e2 / den
    parts = []
    for h in range(WIN_HEADS):
        hs = slice(h * HEAD_DIM, (h + 1) * HEAD_DIM)
        parts.append(w0[:, h:h + 1] * o0_ref[:, hs] + w1[:, h:h + 1] * o1_ref[:, hs] + w2[:, h:h + 1] * o2_ref[:, hs])
    parts.append(om_ref[...])
    cat = jnp.concatenate(parts, axis=1).astype(BF16)
    out_ref[...] = h_ref[...] + jnp.dot(cat, w_ref[...], preferred_element_type=F32)


def _out_a(outs, lses, om, h, w_bf16, tm):
    n, d = h.shape
    gw = WIN_HEADS * HEAD_DIM
    row = lambda c: pl.BlockSpec((tm, c), lambda i: (i, 0))
    return pl.pallas_call(
        _out_a_kernel,
        grid=(n // tm,),
        in_specs=[row(gw)] * 3 + [row(LANES)] * 3 + [row(om.shape[1]), row(d),
                                                    pl.BlockSpec(w_bf16.shape, lambda i: (0, 0))],
        out_specs=row(d),
        out_shape=jax.ShapeDtypeStruct((n, d), F32),
        compiler_params=_params("parallel"),
        name="out_a",
    )(*outs, *lses, om, h, w_bf16)


def _out_b_kernel(a_ref, b_ref, h_ref, w_ref, out_ref):
    ka = a_ref.shape[1]
    acc = jnp.dot(a_ref[...].astype(BF16), w_ref[:ka, :], preferred_element_type=F32)
    acc = acc + jnp.dot(b_ref[...].astype(BF16), w_ref[ka:, :], preferred_element_type=F32)
    out_ref[...] = h_ref[...] + acc


def _out_b(a, b, h, w_bf16, tm):
    n, d = h.shape
    row = lambda c: pl.BlockSpec((tm, c), lambda i: (i, 0))
    return pl.pallas_call(
        _out_b_kernel,
        grid=(n // tm,),
        in_specs=[row(a.shape[1]), row(b.shape[1]), row(d), pl.BlockSpec(w_bf16.shape, lambda i: (0, 0))],
        out_specs=row(d),
        out_shape=jax.ShapeDtypeStruct((n, d), F32),
        compiler_params=_params("parallel"),
        name="out_b",
    )(a, b, h, w_bf16)


def _sb_prompt_kernel(q_ref, k_ref, v_ref, b_ref, o_ref):
    t = SB_TILE
    i = pl.program_id(2)
    heads = q_ref.shape[1] // HEAD_DIM
    head0 = pl.program_id(1) * heads
    row = lax.broadcasted_iota(jnp.int32, (t, t), 0)
    col = lax.broadcasted_iota(jnp.int32, (t, t), 1)
    later = (row > col).astype(BF16)
    strict = col < row
    cols = [slice(h * HEAD_DIM, (h + 1) * HEAD_DIM) for h in range(heads)]
    qs = [(q_ref[:, c] * SCALE).astype(BF16) for c in cols]
    biases = [b_ref[0, head0 + h] for h in range(heads)]

    def run(tiles, carry):
        units = [(h, pl.multiple_of(j * t, t), mask) for h in range(heads) for j, mask in tiles]
        zs = [_dot_t(qs[h], k_ref[pl.ds(s, t), cols[h]]) + biases[h] for h, s, _ in units]
        sps = []
        for (_, _, mask), z in zip(units, zs):
            sp = _softplus(z)
            sps.append(sp if mask is None else jnp.where(mask, sp, 0.0))
        afters = []
        for sp in sps:
            hi = sp.astype(BF16)
            lo = (sp - hi.astype(F32)).astype(BF16)
            afters.append(jnp.dot(hi, later, preferred_element_type=F32)
                          + jnp.dot(lo, later, preferred_element_type=F32))
        carry = list(carry)
        for (h, s, mask), z, sp, after in zip(units, zs, sps, afters):
            acc, cm = carry[h]
            a = jnp.exp(z - sp - after - cm)
            if mask is not None:
                a = jnp.where(mask, a, 0.0)
            acc = acc + jnp.dot(a.astype(BF16), v_ref[pl.ds(s, t), cols[h]], preferred_element_type=F32)
            carry[h] = (acc, cm + (after[:, :1] + sp[:, :1]))
        return tuple(carry)

    has_partner = (i % 2) == 1
    partner_mask = row < jnp.where(has_partner, t, 0)
    init = tuple((jnp.zeros((t, HEAD_DIM), F32), jnp.zeros((t, 1), F32)) for _ in range(heads))
    carry = run([(i, strict), (jnp.maximum(i - 1, 0), partner_mask)], init)
    rest = i - has_partner.astype(jnp.int32)

    def pair(p, c):
        j = rest - 1 - 2 * p
        return run([(j, None), (j - 1, None)], c)

    carry = lax.fori_loop(0, rest // 2, pair, carry)
    for h in range(heads):
        o_ref[:, cols[h]] = carry[h][0]


def _sb_prompt(p_b, kv_bf16, bias, seq, heads_per_step=4):
    n = kv_bf16.shape[0]
    t = SB_TILE
    nq = seq // t
    hw = heads_per_step * HEAD_DIM
    groups = SB_HEADS // heads_per_step
    return pl.pallas_call(
        _sb_prompt_kernel,
        grid=(n // seq, groups, nq),
        in_specs=[pl.BlockSpec((t, hw), lambda b, h, i: (b * nq + i, h)),
                  pl.BlockSpec((seq, hw), lambda b, h, i: (b, h)),
                  pl.BlockSpec((seq, hw), lambda b, h, i: (b, groups + h)),
                  pl.BlockSpec(memory_space=pltpu.SMEM)],
        out_specs=pl.BlockSpec((t, hw), lambda b, h, i: (b * nq + i, h)),
        out_shape=jax.ShapeDtypeStruct((n, SB_HEADS * HEAD_DIM), F32),
        compiler_params=_params("parallel", "parallel", "parallel"),
        name="sb_prompt",
    )(p_b, kv_bf16, kv_bf16, bias.reshape(1, SB_HEADS))


def _sb_step_kernel(pt_ref, q_ref, b_ref, *refs, pages_per_step):
    del pt_ref
    k_refs = refs[:pages_per_step]
    v_refs = refs[pages_per_step:2 * pages_per_step]
    o_ref = refs[2 * pages_per_step]
    acc_ref, cm_ref = refs[2 * pages_per_step + 1:]
    c = pl.program_id(1)
    page = k_refs[0].shape[0] // SB_HEADS

    def head_rows(ref, h):
        return ref[pl.ds(h, page, stride=SB_HEADS), :]

    @pl.when(c == 0)
    def _():
        acc_ref[...] = jnp.zeros_like(acc_ref)
        cm_ref[...] = jnp.zeros_like(cm_ref)

    q = q_ref[0] * SCALE
    bias = b_ref[...]
    lane = lax.broadcasted_iota(jnp.int32, (page, LANES), 1)
    row = lax.broadcasted_iota(jnp.int32, (page, page), 0)
    col = lax.broadcasted_iota(jnp.int32, (page, page), 1)
    later = (col > row).astype(BF16)
    acc = acc_ref[...]
    cm = cm_ref[...]
    zs = []
    for k_ref in k_refs:
        z = jnp.zeros((page, LANES), F32)
        for h in range(SB_HEADS):
            zh = jnp.sum(head_rows(k_ref, h) * q[:, h * HEAD_DIM:(h + 1) * HEAD_DIM], axis=-1, keepdims=True)
            z = jnp.where(lane == h, zh, z)
        zs.append(z + bias)
    sps = [_softplus(z) for z in zs]
    afters = []
    for sp in sps:
        hi = sp.astype(BF16)
        lo = (sp - hi.astype(F32)).astype(BF16)
        afters.append(jnp.dot(later, hi, preferred_element_type=F32) + jnp.dot(later, lo, preferred_element_type=F32))
    for v_ref, z, sp, after in zip(v_refs, zs, sps, afters):
        a = jnp.exp(z - sp - after - cm)
        parts = []
        for h in range(SB_HEADS):
            av = a[:, h:h + 1] * head_rows(v_ref, h)
            parts.append(jnp.sum(av.reshape(page // SUBLANES, SUBLANES, HEAD_DIM), axis=0))
        acc = acc + jnp.concatenate(parts, axis=1)
        cm = cm + jnp.sum(sp, axis=0, keepdims=True)
    acc_ref[...] = acc
    cm_ref[...] = cm

    @pl.when(c == pl.num_programs(1) - 1)
    def _():
        o_ref[0] = jnp.sum(acc, axis=0, keepdims=True)


def _sb_step(p_b, k_pages, v_pages, page_table, bias, pages_per_step=8):
    nseq = p_b.shape[0]
    page = k_pages.shape[1]
    n_pages = page_table.shape[1]
    w = SB_HEADS * HEAD_DIM
    assert n_pages % pages_per_step == 0
    chunks = n_pages // pages_per_step
    assert k_pages.shape[2:] == (SB_HEADS, HEAD_DIM) and SB_HEADS == SUBLANES
    kp = k_pages.reshape(k_pages.shape[0], page * SB_HEADS, HEAD_DIM)
    vp = v_pages.reshape(v_pages.shape[0], page * SB_HEADS, HEAD_DIM)
    bias_row = jnp.zeros((1, LANES), F32).at[0, :SB_HEADS].set(bias)

    def page_spec(u):
        return pl.BlockSpec((None, page * SB_HEADS, HEAD_DIM),
                            lambda s, c, pt: (pt[s, n_pages - 1 - (c * pages_per_step + u)], 0, 0))

    grid_spec = pltpu.PrefetchScalarGridSpec(
        num_scalar_prefetch=1,
        grid=(nseq, chunks),
        in_specs=[pl.BlockSpec((1, 1, w), lambda s, c, pt: (s, 0, 0)),
                  pl.BlockSpec((1, LANES), lambda s, c, pt: (0, 0))]
                 + [page_spec(u) for u in range(pages_per_step)] * 2,
        out_specs=pl.BlockSpec((1, 1, w), lambda s, c, pt: (s, 0, 0)),
        scratch_shapes=[pltpu.VMEM((SUBLANES, w), F32), pltpu.VMEM((1, LANES), F32)],
    )
    q3 = p_b[:, :w].reshape(nseq, 1, w)
    out = pl.pallas_call(
        functools.partial(_sb_step_kernel, pages_per_step=pages_per_step),
        grid_spec=grid_spec,
        out_shape=jax.ShapeDtypeStruct((nseq, 1, w), F32),
        compiler_params=_params("parallel", "arbitrary"),
        name="sb_step",
    )(page_table, q3, bias_row, *([kp] * pages_per_step), *([vp] * pages_per_step))
    return out.reshape(nseq, w)


def _store_row_tiles(ref, val):
    rows = val.shape[0]
    for s in range(SUBLANES):
        ref[pl.ds(s, rows, stride=SUBLANES), :] = val[:, s * LANES:(s + 1) * LANES]


def _load_row_tiles(ref):
    rows = ref.shape[0] // SUBLANES
    return jnp.concatenate([ref[pl.ds(s, rows, stride=SUBLANES), :] for s in range(SUBLANES)], axis=1)


def _row_tile(ref, r):
    return ref.at[pl.ds(pl.multiple_of(r * SUBLANES, SUBLANES), SUBLANES), :]


def _norm_router_kernel(xp_ref, xs_ref, g_ref, wr_ref, br_ref, xn_ref, eid_ref, gate_ref, *, prompt_tiles):
    x = jnp.where(pl.program_id(0) < prompt_tiles, xp_ref[...], xs_ref[...])
    ms = jnp.mean(x * x, axis=-1, keepdims=True)
    xn = x * lax.rsqrt(ms + RMS_EPS) * g_ref[...]
    _store_row_tiles(xn_ref, xn)
    logits = jnp.dot(xn, wr_ref[...], preferred_element_type=F32, precision=lax.Precision.HIGHEST) + br_ref[...]
    lane = lax.broadcasted_iota(jnp.int32, logits.shape, 1).astype(F32)
    first = lambda hit: jnp.min(jnp.where(hit, lane, float(LANES)), axis=-1, keepdims=True)

    lg = jnp.where(lane < MOE_GROUPS, logits, -jnp.inf)
    g_max = jnp.max(lg, axis=-1, keepdims=True)
    top_g = first(lg == g_max)
    p_top = 1.0 / jnp.sum(jnp.exp(lg - g_max), axis=-1, keepdims=True)
    lo = MOE_GROUPS + top_g * EXPERTS_PER_GROUP
    le = jnp.where((lane >= lo) & (lane < lo + EXPERTS_PER_GROUP), logits, -jnp.inf)
    v1 = jnp.max(le, axis=-1, keepdims=True)
    i1 = first(le == v1)
    le2 = jnp.where(lane == i1, -jnp.inf, le)
    v2 = jnp.max(le2, axis=-1, keepdims=True)
    i2 = first(le2 == v2)
    e21 = jnp.exp(v2 - v1)
    g1 = p_top / (1.0 + e21)
    g2 = p_top * e21 / (1.0 + e21)
    eid_ref[...] = jnp.where(lane == 0.0, i1 - MOE_GROUPS, i2 - MOE_GROUPS).astype(jnp.int32)
    gate_ref[...] = jnp.where(lane == 0.0, g1, g2)


def _norm_router(h_p, h_s, g, wr, br, tm):
    n_p, d = h_p.shape
    n_s = h_s.shape[0]
    assert n_p % tm == 0 and n_s % tm == 0 and d == SUBLANES * LANES
    tp, ts = n_p // tm, n_s // tm
    return pl.pallas_call(
        functools.partial(_norm_router_kernel, prompt_tiles=tp),
        grid=(tp + ts,),
        in_specs=[pl.BlockSpec((tm, d), lambda i: (jnp.minimum(i, tp - 1), 0)),
                  pl.BlockSpec((tm, d), lambda i: (jnp.maximum(i - tp, 0), 0)),
                  pl.BlockSpec((1, d), lambda i: (0, 0)),
                  pl.BlockSpec((d, LANES), lambda i: (0, 0)),
                  pl.BlockSpec((1, LANES), lambda i: (0, 0))],
        out_specs=[pl.BlockSpec((tm * SUBLANES, LANES), lambda i: (i, 0)),
                   pl.BlockSpec((tm, LANES), lambda i: (i, 0)),
                   pl.BlockSpec((tm, LANES), lambda i: (i, 0))],
        out_shape=[jax.ShapeDtypeStruct(((n_p + n_s) * SUBLANES, LANES), F32),
                   jax.ShapeDtypeStruct((n_p + n_s, LANES), jnp.int32),
                   jax.ShapeDtypeStruct((n_p + n_s, LANES), F32)],
        compiler_params=_params("parallel"),
        name="norm_router",
    )(h_p, h_s, g.reshape(1, d), wr, br)


def _expert_kernel(be_ref, tok_ref, base_ref, nu_ref, xn_hbm, wg_ref, wu_ref, wd_ref, yr_ref,
                   xbuf, sem, wg_s, wu_s, wd_s):
    i = pl.program_id(0)
    n_used = nu_ref[0]
    rows = MOE_ROWS

    def row_copy(base, slot, r):
        tok = tok_ref[base + r]
        return pltpu.make_async_copy(_row_tile(xn_hbm, tok), _row_tile(xbuf.at[slot], r), sem.at[slot])

    def start_gather(blk, slot):
        base = base_ref[blk]

        def body(r, carry):
            row_copy(base, slot, r).start()
            return carry
        lax.fori_loop(0, rows, body, 0, unroll=GATHER_UNROLL)

    def wait_gather(blk, slot):
        base = base_ref[blk]

        def body(r, carry):
            row_copy(base, slot, r).wait()
            return carry
        lax.fori_loop(0, rows, body, 0, unroll=GATHER_UNROLL)

    @pl.when(jnp.logical_and(i == 0, n_used > 0))
    def _():
        start_gather(0, 0)

    @pl.when(i + 1 < n_used)
    def _():
        start_gather(i + 1, (i + 1) % 2)

    @pl.when(i >= n_used)
    def _():
        yr_ref[...] = jnp.zeros_like(yr_ref)

    @pl.when(i < n_used)
    def _():
        slot = i % 2
        changed = jnp.logical_or(i == 0, be_ref[i] != be_ref[jnp.maximum(i - 1, 0)])

        @pl.when(changed)
        def _():
            wg_s[...] = wg_ref[...].astype(BF16)
            wu_s[...] = wu_ref[...].astype(BF16)
            wd_s[...] = wd_ref[...].astype(BF16)

        wait_gather(i, slot)
        x = _load_row_tiles(xbuf.at[slot]).astype(BF16)
        gate = jnp.dot(x, wg_s[...], preferred_element_type=F32)
        up = jnp.dot(x, wu_s[...], preferred_element_type=F32)
        mid = (gate * jax.nn.sigmoid(gate) * up).astype(BF16)
        y = jnp.dot(mid, wd_s[...], preferred_element_type=F32)
        _store_row_tiles(yr_ref, y)


def _experts(xn_all, blk_exp, sorted_tok, blk_base, n_used, wg, wu, wd, layer):
    n_blk = blk_exp.shape[0]
    n_rows = n_blk * MOE_ROWS
    d, de = wg.shape[2:]
    grid_spec = pltpu.PrefetchScalarGridSpec(
        num_scalar_prefetch=4,
        grid=(n_blk,),
        in_specs=[pl.BlockSpec(memory_space=pl.ANY),
                  pl.BlockSpec((None, None, d, de), lambda i, be, *_: (layer, be[i], 0, 0)),
                  pl.BlockSpec((None, None, d, de), lambda i, be, *_: (layer, be[i], 0, 0)),
                  pl.BlockSpec((None, None, de, d), lambda i, be, *_: (layer, be[i], 0, 0))],
        out_specs=pl.BlockSpec((MOE_ROWS * SUBLANES, LANES), lambda i, *_: (i, 0)),
        scratch_shapes=[pltpu.VMEM((2, MOE_ROWS * SUBLANES, LANES), F32), pltpu.SemaphoreType.DMA((2,)),
                        pltpu.VMEM((d, de), BF16), pltpu.VMEM((d, de), BF16), pltpu.VMEM((de, d), BF16)],
    )
    return pl.pallas_call(
        _expert_kernel,
        grid_spec=grid_spec,
        out_shape=jax.ShapeDtypeStruct((n_rows * SUBLANES, LANES), F32),
        compiler_params=_params("arbitrary"),
        name="moe_experts",
    )(blk_exp, sorted_tok, blk_base, n_used, xn_all, wg, wu, wd)


def _combine_kernel(p0_ref, p1_ref, yr_hbm, g0_ref, g1_ref, h_ref, out_ref, buf, sem, *, tc):
    i = pl.program_id(0)
    nsteps = pl.num_programs(0)

    def row_copy(step, slot, r, which):
        pos_ref = p1_ref if which else p0_ref
        pos = pos_ref[step * tc + r]
        return pltpu.make_async_copy(_row_tile(yr_hbm, pos), _row_tile(buf.at[slot, which], r), sem.at[slot])

    def start_gather(step, slot):
        def body(r, carry):
            row_copy(step, slot, r, 0).start()
            row_copy(step, slot, r, 1).start()
            return carry
        lax.fori_loop(0, tc, body, 0, unroll=GATHER_UNROLL)

    def wait_gather(step, slot):
        def body(r, carry):
            row_copy(step, slot, r, 0).wait()
            row_copy(step, slot, r, 1).wait()
            return carry
        lax.fori_loop(0, tc, body, 0, unroll=GATHER_UNROLL)

    @pl.when(i == 0)
    def _():
        start_gather(0, 0)

    @pl.when(i + 1 < nsteps)
    def _():
        start_gather(i + 1, (i + 1) % 2)

    slot = i % 2
    wait_gather(i, slot)
    y = g0_ref[...] * _load_row_tiles(buf.at[slot, 0]) + g1_ref[...] * _load_row_tiles(buf.at[slot, 1])
    out_ref[...] = h_ref[...] + y


def _combine(yr, pos, gates, h, tc):
    n, d = h.shape
    assert n % tc == 0
    pos0, pos1 = pos[:, 0], pos[:, 1]
    g0, g1 = gates[:, 0:1], gates[:, 1:2]
    grid_spec = pltpu.PrefetchScalarGridSpec(
        num_scalar_prefetch=2,
        grid=(n // tc,),
        in_specs=[pl.BlockSpec(memory_space=pl.ANY),
                  pl.BlockSpec((tc, 1), lambda i, p0, p1: (i, 0)),
                  pl.BlockSpec((tc, 1), lambda i, p0, p1: (i, 0)),
                  pl.BlockSpec((tc, d), lambda i, p0, p1: (i, 0))],
        out_specs=pl.BlockSpec((tc, d), lambda i, p0, p1: (i, 0)),
        scratch_shapes=[pltpu.VMEM((2, 2, tc * SUBLANES, LANES), F32), pltpu.SemaphoreType.DMA((2,))],
    )
    return pl.pallas_call(
        functools.partial(_combine_kernel, tc=tc),
        grid_spec=grid_spec,
        out_shape=jax.ShapeDtypeStruct((n, d), F32),
        compiler_params=_params("arbitrary"),
        name="moe_combine",
    )(pos0, pos1, yr, g0, g1, h)


def _dispatch(eid):
    n = eid.shape[0]
    n_assign = 2 * n
    e_flat = eid.reshape(n_assign)
    onehot = (e_flat[:, None] == jnp.arange(N_EXPERTS, dtype=jnp.int32)[None, :]).astype(jnp.int32)
    csum = jnp.cumsum(onehot, axis=0)
    counts = csum[-1]
    rank = jnp.sum((csum - onehot) * onehot, axis=1)
    padded = (counts + MOE_ROWS - 1) // MOE_ROWS * MOE_ROWS
    pend = jnp.cumsum(padded)
    pstart = pend - padded
    dest = (pstart[e_flat] + rank).astype(jnp.int32)
    n_blk = (n_assign + N_EXPERTS * (MOE_ROWS - 1) + MOE_ROWS - 1) // MOE_ROWS
    blk_start = jnp.arange(n_blk, dtype=jnp.int32) * MOE_ROWS
    blk_exp = jnp.minimum(jnp.sum(blk_start[:, None] >= pend[None, :], axis=1), N_EXPERTS - 1).astype(jnp.int32)
    n_used = (pend[-1] // MOE_ROWS).astype(jnp.int32).reshape(1)
    order = jnp.argsort(e_flat, stable=True).astype(jnp.int32)
    sorted_tok = jnp.concatenate([order // 2, jnp.zeros((MOE_ROWS,), jnp.int32)])
    start = jnp.cumsum(counts) - counts
    blk_base = jnp.clip(start[blk_exp] + blk_start - pstart[blk_exp], 0, n_assign).astype(jnp.int32)
    return sorted_tok, blk_exp, blk_base, n_used, dest.reshape(n, 2)


def _hier_moe(h_p, h_s, g, wr, br, wg, wu, wd, layer):
    n_p, d = h_p.shape
    n_s = h_s.shape[0]
    xn_all, eid, gates = _norm_router(h_p, h_s, g, wr, br, math.gcd(128, n_s))
    eid, gates = eid[:, :2], gates[:, :2]
    sorted_tok, blk_exp, blk_base, n_used, pos = _dispatch(eid)
    yr = _experts(xn_all, blk_exp, sorted_tok, blk_base, n_used, wg, wu, wd, layer)
    out_p = _combine(yr, pos[:n_p], gates[:n_p], h_p, 128)
    out_s = _combine(yr, pos[n_p:], gates[n_p:], h_s, min(128, n_s))
    return out_p, out_s


def kernel(x_prompt, x_sample, mem_prompt, cache_win_k0, cache_win_v0, cache_win_k1, cache_win_v1, cache_win_k2, cache_win_v2, cache_mem_k, cache_mem_v, cache_k_pages, cache_v_pages, page_table, g_mix, g_mem, g_ffn, g_kv, g_final, w_in_a, w_out_a, w_in_b, w_out_b, b_sb, w_kv, w_mem_k, w_mem_v, w_router_group, b_router_group, w_router_expert, b_router_expert, w_exp_gate, w_exp_up, w_exp_down):
    bp, seq, d = x_prompt.shape
    bs, t_new, _ = x_sample.shape
    assert t_new == 1, "one new token per sample sequence"
    depth = g_mix.shape[0]
    n_a = w_in_a.shape[0]
    n_p = bp * seq
    tm_p = 512
    tm_s = bs
    mem_tok = mem_prompt.shape[1]
    gw = WIN_HEADS * HEAD_DIM
    win_k_cache = (cache_win_k0, cache_win_k1, cache_win_k2)
    win_v_cache = (cache_win_v0, cache_win_v1, cache_win_v2)

    h = x_prompt.reshape(n_p, d)
    hs = x_sample.reshape(bs, d)
    mem2 = mem_prompt.reshape(bp * mem_tok, d)

    wkp = [[] for _ in WIN_GROUPS]
    wvp = [[] for _ in WIN_GROUPS]
    wks = [[] for _ in WIN_GROUPS]
    wvs = [[] for _ in WIN_GROUPS]
    mkp_list, mvp_list = [], []

    for l in range(depth):
        w_mem = jnp.concatenate([w_mem_k[l], w_mem_v[l]], axis=1).astype(BF16)
        memkv = _rms_matmul(mem2, g_mem[l], w_mem, mem2.shape[0], w_mem.shape[1])
        mw = MEM_HEADS * HEAD_DIM
        mkp_list.append(memkv[:, :mw].reshape(bp, mem_tok, MEM_HEADS, HEAD_DIM))
        mvp_list.append(memkv[:, mw:].reshape(bp, mem_tok, MEM_HEADS, HEAD_DIM))
        if l < n_a:
            w_in = w_in_a[l].astype(BF16)
            pa = _rms_matmul(h, g_mix[l], w_in, tm_p, 2560)
            pa_s = _rms_matmul(hs, g_mix[l], w_in, tm_s, 2560)
            outs, lses = [], []
            for g, (win, dil) in enumerate(WIN_GROUPS):
                o, lse = _win_prompt(pa, g, seq)
                outs.append(o)
                lses.append(lse)
                keep = min(win, seq)
                tail = pa.reshape(bp, seq, -1)[:, seq - keep:]
                ka = tail[:, :, (N_WIN_GROUPS + g) * gw:(N_WIN_GROUPS + g + 1) * gw]
                va = tail[:, :, (2 * N_WIN_GROUPS + g) * gw:(2 * N_WIN_GROUPS + g + 1) * gw]
                wkp[g].append(ka.reshape(bp, keep, WIN_HEADS, HEAD_DIM))
                wvp[g].append(va.reshape(bp, keep, WIN_HEADS, HEAD_DIM))
                kn = pa_s[:, (N_WIN_GROUPS + g) * gw:(N_WIN_GROUPS + g + 1) * gw].reshape(bs, 1, WIN_HEADS, HEAD_DIM)
                vn = pa_s[:, (2 * N_WIN_GROUPS + g) * gw:(2 * N_WIN_GROUPS + g + 1) * gw].reshape(bs, 1, WIN_HEADS, HEAD_DIM)
                keep_s = min(win, win_k_cache[g].shape[2] + 1)
                wks[g].append(jnp.concatenate([win_k_cache[g][l], kn], axis=1)[:, -keep_s:])
                wvs[g].append(jnp.concatenate([win_v_cache[g][l], vn], axis=1)[:, -keep_s:])
            outs_s, lses_s = zip(*[_win_step(pa_s, g, win_k_cache[g], win_v_cache[g], l)
                                   for g in range(N_WIN_GROUPS)])
            q_mem_block = 3 * N_WIN_GROUPS
            om = _mem_prompt(pa, q_mem_block, memkv, seq, tm_p)
            om_s = _mem_step(pa_s, q_mem_block, cache_mem_k, cache_mem_v, l)
            w_out = w_out_a[l].astype(BF16)
            h = _out_a(outs, lses, om, h, w_out, tm_p)
            hs = _out_a(outs_s, lses_s, om_s, hs, w_out, tm_s)
        else:
            lb = l - n_a
            if l == n_a:
                wkv = w_kv.astype(BF16)
                k_p, v_p, kv_p16 = _kv_proj(h, g_kv, wkv, tm_p)
                k_s, v_s, _ = _kv_proj(hs, g_kv, wkv, tm_s)
            w_in = w_in_b[lb].astype(BF16)
            pb = _rms_matmul(h, g_mix[l], w_in, tm_p, w_in.shape[1])
            pb_s = _rms_matmul(hs, g_mix[l], w_in, tm_s, w_in.shape[1])
            osb = _sb_prompt(pb, kv_p16, b_sb[lb], seq)
            osb_s = _sb_step(pb_s, cache_k_pages, cache_v_pages, page_table, b_sb[lb])
            q_mem_block = SB_HEADS * HEAD_DIM // (MEM_HEADS * HEAD_DIM)
            om = _mem_prompt(pb, q_mem_block, memkv, seq, tm_p)
            om_s = _mem_step(pb_s, q_mem_block, cache_mem_k, cache_mem_v, l)
            w_out = w_out_b[lb].astype(BF16)
            h = _out_b(osb, om, h, w_out, tm_p)
            hs = _out_b(osb_s, om_s, hs, w_out, tm_s)
        wr = jnp.zeros((d, LANES), F32).at[:, :MOE_GROUPS].set(w_router_group[l])
        wr = wr.at[:, MOE_GROUPS:MOE_GROUPS + N_EXPERTS].set(w_router_expert[l])
        br = jnp.zeros((1, LANES), F32).at[0, :MOE_GROUPS].set(b_router_group[l])
        br = br.at[0, MOE_GROUPS:MOE_GROUPS + N_EXPERTS].set(b_router_expert[l])
        h, hs = _hier_moe(h, hs, g_ffn[l], wr, br, w_exp_gate, w_exp_up, w_exp_down, l)

    y_prompt = _rms(h, g_final, tm_p).reshape(bp, seq, d)
    y_sample = _rms(hs, g_final, tm_s).reshape(bs, 1, d)
    stack = lambda zs: jnp.stack(zs, axis=0)
    kv_k_p = k_p.reshape(bp, seq, SB_HEADS, HEAD_DIM)
    kv_v_p = v_p.reshape(bp, seq, SB_HEADS, HEAD_DIM)
    kv_k_s = k_s.reshape(bs, 1, SB_HEADS, HEAD_DIM)
    kv_v_s = v_s.reshape(bs, 1, SB_HEADS, HEAD_DIM)
    return (y_prompt, y_sample,
            stack(wkp[0]), stack(wvp[0]), stack(wkp[1]), stack(wvp[1]), stack(wkp[2]), stack(wvp[2]),
            stack(wks[0]), stack(wvs[0]), stack(wks[1]), stack(wvs[1]), stack(wks[2]), stack(wvs[2]),
            stack(mkp_list), stack(mvp_list),
            kv_k_p, kv_v_p, kv_k_s, kv_v_s)
```

```python
import functools
import math

import jax
import jax.numpy as jnp
from jax import lax
from jax.experimental import pallas as pl
from jax.experimental.pallas import tpu as pltpu

HEAD_DIM = 128
LANES = 128
SUBLANES = 8
GATHER_UNROLL = 8
RMS_EPS = 1e-6
WIN_GROUPS = ((128, 1), (512, 4), (2048, 16))
N_WIN_GROUPS = 3
WIN_HEADS = 4
WIN_BLOCK = 128
ALIBI_MAX_BIAS = 8.0
SB_HEADS = 8
MEM_HEADS = 4
MOE_GROUPS = 4
EXPERTS_PER_GROUP = 8
N_EXPERTS = MOE_GROUPS * EXPERTS_PER_GROUP
MOE_ROWS = 256
SB_TILE = 256
SCALE = HEAD_DIM ** -0.5
VMEM_LIMIT = 56 * 1024 * 1024

F32 = jnp.float32
BF16 = jnp.bfloat16


def _alibi_slope(g, h):
    n = N_WIN_GROUPS * WIN_HEADS
    return 2.0 ** (-ALIBI_MAX_BIAS * (g * WIN_HEADS + h + 1) / n)


def _params(*sem):
    return pltpu.CompilerParams(dimension_semantics=sem, vmem_limit_bytes=VMEM_LIMIT)


def _dot_t(a, b):
    return lax.dot_general(a, b, (((1,), (1,)), ((), ())), preferred_element_type=F32)


def _softplus(z):
    return jnp.maximum(z, 0.0) + jnp.log(1.0 + jnp.exp(-jnp.abs(z)))


def _rms_matmul_kernel(x_ref, g_ref, w_ref, o_ref, *maybe_bf16_ref):
    x = x_ref[...]
    ms = jnp.mean(x * x, axis=-1, keepdims=True)
    xn = (x * lax.rsqrt(ms + RMS_EPS) * g_ref[...]).astype(BF16)
    y = jnp.dot(xn, w_ref[...], preferred_element_type=F32)
    o_ref[...] = y
    for r in maybe_bf16_ref:
        r[...] = y.astype(BF16)


def _rms_matmul(x, g, w_bf16, tm, tn, also_bf16=False):
    m, d = x.shape
    n = w_bf16.shape[1]
    assert m % tm == 0 and n % tn == 0
    out_spec = pl.BlockSpec((tm, tn), lambda j, i: (i, j))
    out_specs, out_shape = out_spec, jax.ShapeDtypeStruct((m, n), F32)
    if also_bf16:
        out_specs, out_shape = [out_spec, out_spec], [out_shape, jax.ShapeDtypeStruct((m, n), BF16)]
    return pl.pallas_call(
        _rms_matmul_kernel,
        grid=(n // tn, m // tm),
        in_specs=[pl.BlockSpec((tm, d), lambda j, i: (i, 0)),
                  pl.BlockSpec((1, d), lambda j, i: (0, 0)),
                  pl.BlockSpec((d, tn), lambda j, i: (0, j))],
        out_specs=out_specs,
        out_shape=out_shape,
        compiler_params=_params("parallel", "parallel"),
        name="rms_matmul",
    )(x, g.reshape(1, d), w_bf16)


def _kv_proj_kernel(x_ref, g_ref, w_ref, k_ref, v_ref, kv16_ref):
    x = x_ref[...]
    ms = jnp.mean(x * x, axis=-1, keepdims=True)
    xn = (x * lax.rsqrt(ms + RMS_EPS) * g_ref[...]).astype(BF16)
    y = jnp.dot(xn, w_ref[...], preferred_element_type=F32)
    half = y.shape[1] // 2
    _store_row_tiles(k_ref, y[:, :half])
    _store_row_tiles(v_ref, y[:, half:])
    kv16_ref[...] = y.astype(BF16)


def _kv_proj(x, g, w_bf16, tm):
    m, d = x.shape
    n = w_bf16.shape[1]
    assert m % tm == 0 and n == 2 * SB_HEADS * HEAD_DIM and SB_HEADS == SUBLANES
    tiles = pl.BlockSpec((tm * SUBLANES, LANES), lambda i: (i, 0))
    return pl.pallas_call(
        _kv_proj_kernel,
        grid=(m // tm,),
        in_specs=[pl.BlockSpec((tm, d), lambda i: (i, 0)),
                  pl.BlockSpec((1, d), lambda i: (0, 0)),
                  pl.BlockSpec((d, n), lambda i: (0, 0))],
        out_specs=[tiles, tiles, pl.BlockSpec((tm, n), lambda i: (i, 0))],
        out_shape=[jax.ShapeDtypeStruct((m * SUBLANES, LANES), F32)] * 2 + [jax.ShapeDtypeStruct((m, n), BF16)],
        compiler_params=_params("parallel"),
        name="kv_proj",
    )(x, g.reshape(1, d), w_bf16)


def _rms_kernel(x_ref, g_ref, o_ref):
    x = x_ref[...]
    ms = jnp.mean(x * x, axis=-1, keepdims=True)
    o_ref[...] = x * lax.rsqrt(ms + RMS_EPS) * g_ref[...]


def _rms(x, g, tm):
    m, d = x.shape
    return pl.pallas_call(
        _rms_kernel,
        grid=(m // tm,),
        in_specs=[pl.BlockSpec((tm, d), lambda i: (i, 0)), pl.BlockSpec((1, d), lambda i: (0, 0))],
        out_specs=pl.BlockSpec((tm, d), lambda i: (i, 0)),
        out_shape=jax.ShapeDtypeStruct((m, d), F32),
        compiler_params=_params("parallel"),
        name="rms_final",
    )(x, g.reshape(1, d))


def _win_prompt_kernel(q_ref, kp_ref, kc_ref, vp_ref, vc_ref, o_ref, lse_ref, *, slopes, blocks_per_seq):
    first = (pl.program_id(0) % blocks_per_seq) == 0
    blk = WIN_BLOCK
    qi = lax.broadcasted_iota(jnp.int32, (blk, 2 * blk), 0)
    kj = lax.broadcasted_iota(jnp.int32, (blk, 2 * blk), 1)
    rel = qi + blk - kj
    valid = (rel >= 0) & (rel <= blk) & ((kj >= blk) | jnp.logical_not(first))
    relf = rel.astype(F32)
    lane = lax.broadcasted_iota(jnp.int32, (blk, LANES), 1)
    lse_all = jnp.zeros((blk, LANES), F32)
    for h in range(WIN_HEADS):
        hs = slice(h * HEAD_DIM, (h + 1) * HEAD_DIM)
        q = q_ref[:, hs].astype(BF16)
        k = jnp.concatenate([kp_ref[:, hs], kc_ref[:, hs]], axis=0).astype(BF16)
        v = jnp.concatenate([vp_ref[:, hs], vc_ref[:, hs]], axis=0).astype(BF16)
        s = _dot_t(q, k) * SCALE
        s = jnp.where(valid, s - slopes[h] * relf, -jnp.inf)
        m = jnp.max(s, axis=-1, keepdims=True)
        p = jnp.exp(s - m)
        den = jnp.sum(p, axis=-1, keepdims=True)
        o = jnp.dot(p.astype(BF16), v, preferred_element_type=F32)
        o_ref[:, hs] = o / den
        lse_all = jnp.where(lane == h, m + jnp.log(den), lse_all)
    lse_ref[...] = lse_all


def _win_prompt_strided_kernel(slope_ref, q_ref, kp_ref, kc_ref, vp_ref, vc_ref, o_ref, lse_ref, *,
                               g, chunks_per_seq):
    dil = WIN_GROUPS[g][1]
    blk = WIN_BLOCK
    h = pl.program_id(1)
    first = (pl.program_id(0) % chunks_per_seq) == 0
    slope = slope_ref[g * WIN_HEADS + h] * dil
    qi = lax.broadcasted_iota(jnp.int32, (blk, 2 * blk), 0)
    kj = lax.broadcasted_iota(jnp.int32, (blk, 2 * blk), 1)
    rel = qi + blk - kj
    valid = (rel >= 0) & (rel <= blk) & ((kj >= blk) | jnp.logical_not(first))
    bias = slope * rel.astype(F32)
    lane = lax.broadcasted_iota(jnp.int32, (blk, LANES), 1)

    @pl.when(h == 0)
    def _():
        lse_ref[...] = jnp.zeros_like(lse_ref)

    def body(r, carry):
        sub = pl.ds(r, blk, stride=dil)
        q = q_ref[sub, :].astype(BF16)
        k = jnp.concatenate([kp_ref[sub, :], kc_ref[sub, :]], axis=0).astype(BF16)
        v = jnp.concatenate([vp_ref[sub, :], vc_ref[sub, :]], axis=0).astype(BF16)
        s = _dot_t(q, k) * SCALE
        s = jnp.where(valid, s - bias, -jnp.inf)
        m = jnp.max(s, axis=-1, keepdims=True)
        p = jnp.exp(s - m)
        den = jnp.sum(p, axis=-1, keepdims=True)
        o_ref[sub, :] = jnp.dot(p.astype(BF16), v, preferred_element_type=F32) / den
        lse_ref[sub, :] = jnp.where(lane == h, m + jnp.log(den), lse_ref[sub, :])
        return carry

    lax.fori_loop(0, dil, body, 0)


def _win_prompt_strided(p_a, g, seq):
    n, width = p_a.shape
    _, dil = WIN_GROUPS[g]
    rows = WIN_BLOCK * dil
    assert seq % rows == 0 and n % seq == 0
    gw = WIN_HEADS * HEAD_DIM
    k_col, v_col = N_WIN_GROUPS * WIN_HEADS, 2 * N_WIN_GROUPS * WIN_HEADS
    slopes = jnp.asarray([_alibi_slope(gg, hh) for gg in range(N_WIN_GROUPS) for hh in range(WIN_HEADS)], F32)

    def spec(col0, prev):
        if prev:
            return pl.BlockSpec((rows, HEAD_DIM), lambda c, h: (jnp.maximum(c - 1, 0), col0 + g * WIN_HEADS + h))
        return pl.BlockSpec((rows, HEAD_DIM), lambda c, h: (c, col0 + g * WIN_HEADS + h))

    return pl.pallas_call(
        functools.partial(_win_prompt_strided_kernel, g=g, chunks_per_seq=seq // rows),
        grid=(n // rows, WIN_HEADS),
        in_specs=[pl.BlockSpec(memory_space=pltpu.SMEM),
                  spec(0, False), spec(k_col, True), spec(k_col, False), spec(v_col, True), spec(v_col, False)],
        out_specs=[pl.BlockSpec((rows, HEAD_DIM), lambda c, h: (c, h)),
                   pl.BlockSpec((rows, LANES), lambda c, h: (c, 0))],
        out_shape=[jax.ShapeDtypeStruct((n, gw), F32), jax.ShapeDtypeStruct((n, LANES), F32)],
        compiler_params=_params("parallel", "arbitrary"),
        name="win_prompt_g%d" % g,
    )(slopes, p_a, p_a, p_a, p_a, p_a)


def _win_prompt(p_a, g, seq):
    n, width = p_a.shape
    _, dil = WIN_GROUPS[g]
    if dil > 1:
        return _win_prompt_strided(p_a, g, seq)
    sub = seq // dil
    assert sub % WIN_BLOCK == 0 and n % seq == 0
    gw = WIN_HEADS * HEAD_DIM
    cb = width // gw
    nq = N_WIN_GROUPS
    pa_r = p_a.reshape(n // dil, dil * width)
    nb = n // dil // WIN_BLOCK
    blocks_per_seq = sub // WIN_BLOCK
    slopes = tuple(_alibi_slope(g, h) * dil for h in range(WIN_HEADS))

    def spec(col, prev):
        if prev:
            return pl.BlockSpec((WIN_BLOCK, gw), lambda ib, r: (jnp.maximum(ib - 1, 0), r * cb + col))
        return pl.BlockSpec((WIN_BLOCK, gw), lambda ib, r: (ib, r * cb + col))

    o, lse = pl.pallas_call(
        functools.partial(_win_prompt_kernel, slopes=slopes, blocks_per_seq=blocks_per_seq),
        grid=(nb, dil),
        in_specs=[spec(g, False), spec(nq + g, True), spec(nq + g, False),
                  spec(2 * nq + g, True), spec(2 * nq + g, False)],
        out_specs=[pl.BlockSpec((WIN_BLOCK, gw), lambda ib, r: (ib, r)),
                   pl.BlockSpec((WIN_BLOCK, LANES), lambda ib, r: (ib, r))],
        out_shape=[jax.ShapeDtypeStruct((n // dil, dil * gw), F32),
                   jax.ShapeDtypeStruct((n // dil, dil * LANES), F32)],
        compiler_params=_params("parallel", "parallel"),
        name="win_prompt_g%d" % g,
    )(pa_r, pa_r, pa_r, pa_r, pa_r)
    return o.reshape(n, gw), lse.reshape(n, LANES)


def _win_step_kernel(q_ref, kn_ref, vn_ref, bias_ref, kc_ref, vc_ref, o_ref, l_ref):
    q, kn, vn = q_ref[...], kn_ref[...], vn_ref[...]
    kc, vc = kc_ref[...], vc_ref[...]
    s_c = jnp.sum(kc * q[:, None], axis=-1, keepdims=True) * SCALE
    s_c = s_c - bias_ref[...][None, :, :, :1]
    s_n = jnp.sum(kn * q, axis=-1, keepdims=True) * SCALE
    m = jnp.maximum(jnp.max(s_c, axis=1), s_n)
    p_c = jnp.exp(s_c - m[:, None])
    p_n = jnp.exp(s_n - m)
    den = jnp.sum(p_c, axis=1) + p_n
    o_ref[...] = (jnp.sum(p_c * vc, axis=1) + p_n * vn) / den
    l_ref[...] = jnp.broadcast_to(m + jnp.log(den), l_ref.shape)


def _win_step(pa_s, g, k_cache, v_cache, layer):
    nseq, width = pa_s.shape
    gw = WIN_HEADS * HEAD_DIM
    win, dil = WIN_GROUPS[g]
    assert k_cache.shape[2] == win, "cached window rows must cover the whole window"
    bb = 8
    heads = pa_s.reshape(nseq, width // HEAD_DIM, HEAD_DIM)
    pick = lambda c: heads[:, (c * N_WIN_GROUPS + g) * WIN_HEADS:(c * N_WIN_GROUPS + g + 1) * WIN_HEADS]
    steps = (WIN_BLOCK - jnp.arange(WIN_BLOCK, dtype=F32))[:, None]
    slopes = jnp.asarray([_alibi_slope(g, h) * dil for h in range(WIN_HEADS)], F32)[None, :]
    bias = jnp.broadcast_to((steps * slopes)[:, :, None], (WIN_BLOCK, WIN_HEADS, HEAD_DIM))
    tok_spec = pl.BlockSpec((bb, WIN_HEADS, HEAD_DIM), lambda i: (i, 0, 0))
    cache_spec = pl.BlockSpec((None, bb, WIN_BLOCK, None, WIN_HEADS, HEAD_DIM), lambda i: (layer, i, 0, 0, 0, 0))
    split = lambda c: c.reshape(c.shape[0], nseq, WIN_BLOCK, dil, WIN_HEADS, HEAD_DIM)
    o, lse = pl.pallas_call(
        _win_step_kernel,
        grid=(nseq // bb,),
        in_specs=[tok_spec, tok_spec, tok_spec,
                  pl.BlockSpec((WIN_BLOCK, WIN_HEADS, HEAD_DIM), lambda i: (0, 0, 0)), cache_spec, cache_spec],
        out_specs=[tok_spec, tok_spec],
        out_shape=[jax.ShapeDtypeStruct((nseq, WIN_HEADS, HEAD_DIM), F32)] * 2,
        compiler_params=_params("parallel"),
        name="win_step_g%d" % g,
    )(pick(0), pick(1), pick(2), bias, split(k_cache), split(v_cache))
    lse = jnp.pad(lse[:, :, 0], ((0, 0), (0, LANES - WIN_HEADS)))
    return o.reshape(nseq, gw), lse


def _mem_prompt_kernel(q_ref, mk_ref, mv_ref, o_ref):
    for h in range(MEM_HEADS):
        hs = slice(h * HEAD_DIM, (h + 1) * HEAD_DIM)
        q = q_ref[:, hs].astype(BF16)
        s = _dot_t(q, mk_ref[:, hs].astype(BF16)) * SCALE
        m = jnp.max(s, axis=-1, keepdims=True)
        p = jnp.exp(s - m)
        den = jnp.sum(p, axis=-1, keepdims=True)
        o = jnp.dot(p.astype(BF16), mv_ref[:, hs].astype(BF16), preferred_element_type=F32)
        o_ref[:, hs] = o / den


def _mem_prompt(p, q_col_block, memkv, seq, tm):
    n = p.shape[0]
    mw = MEM_HEADS * HEAD_DIM
    mtok = memkv.shape[0] // (n // seq)
    per_seq = seq // tm
    return pl.pallas_call(
        _mem_prompt_kernel,
        grid=(n // tm,),
        in_specs=[pl.BlockSpec((tm, mw), lambda i: (i, q_col_block)),
                  pl.BlockSpec((mtok, mw), lambda i: (i // per_seq, 0)),
                  pl.BlockSpec((mtok, mw), lambda i: (i // per_seq, 1))],
        out_specs=pl.BlockSpec((tm, mw), lambda i: (i, 0)),
        out_shape=jax.ShapeDtypeStruct((n, mw), F32),
        compiler_params=_params("parallel"),
        name="mem_prompt",
    )(p, memkv, memkv)


def _mem_step_kernel(q_ref, mk_ref, mv_ref, o_ref):
    q = q_ref[...]
    s = jnp.sum(mk_ref[...] * q[:, None], axis=-1, keepdims=True) * SCALE
    m = jnp.max(s, axis=1, keepdims=True)
    p = jnp.exp(s - m)
    den = jnp.sum(p, axis=1)
    o_ref[...] = jnp.sum(p * mv_ref[...], axis=1) / den


def _mem_step(p, q_col_block, mk, mv, layer):
    nseq = p.shape[0]
    mw = MEM_HEADS * HEAD_DIM
    mtok = mk.shape[2]
    bb = 8
    cache_spec = pl.BlockSpec((None, bb, mtok, MEM_HEADS, HEAD_DIM), lambda i: (layer, i, 0, 0, 0))
    q = p[:, q_col_block * mw:(q_col_block + 1) * mw].reshape(nseq, MEM_HEADS, HEAD_DIM)
    tok_spec = pl.BlockSpec((bb, MEM_HEADS, HEAD_DIM), lambda i: (i, 0, 0))
    return pl.pallas_call(
        _mem_step_kernel,
        grid=(nseq // bb,),
        in_specs=[tok_spec, cache_spec, cache_spec],
        out_specs=tok_spec,
        out_shape=jax.ShapeDtypeStruct((nseq, MEM_HEADS, HEAD_DIM), F32),
        compiler_params=_params("parallel"),
        name="mem_step",
    )(q, mk, mv).reshape(nseq, mw)


def _out_a_kernel(o0_ref, o1_ref, o2_ref, l0_ref, l1_ref, l2_ref, om_ref, h_ref, w_ref, out_ref):
    l0, l1, l2 = l0_ref[...], l1_ref[...], l2_ref[...]
    mx = jnp.maximum(jnp.maximum(l0, l1), l2)
    e0, e1, e2 = jnp.exp(l0 - mx), jnp.exp(l1 - mx), jnp.exp(l2 - mx)
    den = e0 + e1 + e2
    w0, w1, w2 = e0 / den, e1 / den, e2 / den
    parts = []
    for h in range(WIN_HEADS):
        hs = slice(h * HEAD_DIM, (h + 1) * HEAD_DIM)
        parts.append(w0[:, h:h + 1] * o0_ref[:, hs] + w1[:, h:h + 1] * o1_ref[:, hs] + w2[:, h:h + 1] * o2_ref[:, hs])
    parts.append(om_ref[...])
    cat = jnp.concatenate(parts, axis=1).astype(BF16)
    out_ref[...] = h_ref[...] + jnp.dot(cat, w_ref[...], preferred_element_type=F32)


def _out_a(outs, lses, om, h, w_bf16, tm):
    n, d = h.shape
    gw = WIN_HEADS * HEAD_DIM
    row = lambda c: pl.BlockSpec((tm, c), lambda i: (i, 0))
    return pl.pallas_call(
        _out_a_kernel,
        grid=(n // tm,),
        in_specs=[row(gw)] * 3 + [row(LANES)] * 3 + [row(om.shape[1]), row(d),
                                                    pl.BlockSpec(w_bf16.shape, lambda i: (0, 0))],
        out_specs=row(d),
        out_shape=jax.ShapeDtypeStruct((n, d), F32),
        compiler_params=_params("parallel"),
        name="out_a",
    )(*outs, *lses, om, h, w_bf16)


def _out_b_kernel(a_ref, b_ref, h_ref, w_ref, out_ref):
    ka = a_ref.shape[1]
    acc = jnp.dot(a_ref[...].astype(BF16), w_ref[:ka, :], preferred_element_type=F32)
    acc = acc + jnp.dot(b_ref[...].astype(BF16), w_ref[ka:, :], preferred_element_type=F32)
    out_ref[...] = h_ref[...] + acc


def _out_b(a, b, h, w_bf16, tm):
    n, d = h.shape
    row = lambda c: pl.BlockSpec((tm, c), lambda i: (i, 0))
    return pl.pallas_call(
        _out_b_kernel,
        grid=(n // tm,),
        in_specs=[row(a.shape[1]), row(b.shape[1]), row(d), pl.BlockSpec(w_bf16.shape, lambda i: (0, 0))],
        out_specs=row(d),
        out_shape=jax.ShapeDtypeStruct((n, d), F32),
        compiler_params=_params("parallel"),
        name="out_b",
    )(a, b, h, w_bf16)


def _sb_prompt_kernel(q_ref, k_ref, v_ref, b_ref, o_ref):
    t = SB_TILE
    i = pl.program_id(2)
    heads = q_ref.shape[1] // HEAD_DIM
    head0 = pl.program_id(1) * heads
    row = lax.broadcasted_iota(jnp.int32, (t, t), 0)
    col = lax.broadcasted_iota(jnp.int32, (t, t), 1)
    later = (row > col).astype(BF16)
    strict = col < row
    cols = [slice(h * HEAD_DIM, (h + 1) * HEAD_DIM) for h in range(heads)]
    qs = [(q_ref[:, c] * SCALE).astype(BF16) for c in cols]
    biases = [b_ref[0, head0 + h] for h in range(heads)]

    def run(tiles, carry):
        units = [(h, pl.multiple_of(j * t, t), mask) for h in range(heads) for j, mask in tiles]
        zs = [_dot_t(qs[h], k_ref[pl.ds(s, t), cols[h]]) + biases[h] for h, s, _ in units]
        sps = []
        for (_, _, mask), z in zip(units, zs):
            sp = _softplus(z)
            sps.append(sp if mask is None else jnp.where(mask, sp, 0.0))
        afters = []
        for sp in sps:
            hi = sp.astype(BF16)
            lo = (sp - hi.astype(F32)).astype(BF16)
            afters.append(jnp.dot(hi, later, preferred_element_type=F32)
                          + jnp.dot(lo, later, preferred_element_type=F32))
        carry = list(carry)
        for (h, s, mask), z, sp, after in zip(units, zs, sps, afters):
            acc, cm = carry[h]
            a = jnp.exp(z - sp - after - cm)
            if mask is not None:
                a = jnp.where(mask, a, 0.0)
            acc = acc + jnp.dot(a.astype(BF16), v_ref[pl.ds(s, t), cols[h]], preferred_element_type=F32)
            carry[h] = (acc, cm + (after[:, :1] + sp[:, :1]))
        return tuple(carry)

    has_partner = (i % 2) == 1
    partner_mask = row < jnp.where(has_partner, t, 0)
    init = tuple((jnp.zeros((t, HEAD_DIM), F32), jnp.zeros((t, 1), F32)) for _ in range(heads))
    carry = run([(i, strict), (jnp.maximum(i - 1, 0), partner_mask)], init)
    rest = i - has_partner.astype(jnp.int32)

    def pair(p, c):
        j = rest - 1 - 2 * p
        return run([(j, None), (j - 1, None)], c)

    carry = lax.fori_loop(0, rest // 2, pair, carry)
    for h in range(heads):
        o_ref[:, cols[h]] = carry[h][0]


def _sb_prompt(p_b, kv_bf16, bias, seq, heads_per_step=4):
    n = kv_bf16.shape[0]
    t = SB_TILE
    nq = seq // t
    hw = heads_per_step * HEAD_DIM
    groups = SB_HEADS // heads_per_step
    return pl.pallas_call(
        _sb_prompt_kernel,
        grid=(n // seq, groups, nq),
        in_specs=[pl.BlockSpec((t, hw), lambda b, h, i: (b * nq + i, h)),
                  pl.BlockSpec((seq, hw), lambda b, h, i: (b, h)),
                  pl.BlockSpec((seq, hw), lambda b, h, i: (b, groups + h)),
                  pl.BlockSpec(memory_space=pltpu.SMEM)],
        out_specs=pl.BlockSpec((t, hw), lambda b, h, i: (b * nq + i, h)),
        out_shape=jax.ShapeDtypeStruct((n, SB_HEADS * HEAD_DIM), F32),
        compiler_params=_params("parallel", "parallel", "parallel"),
        name="sb_prompt",
    )(p_b, kv_bf16, kv_bf16, bias.reshape(1, SB_HEADS))


def _sb_step_kernel(pt_ref, q_ref, b_ref, *refs, pages_per_step):
    del pt_ref
    k_refs = refs[:pages_per_step]
    v_refs = refs[pages_per_step:2 * pages_per_step]
    o_ref = refs[2 * pages_per_step]
    acc_ref, cm_ref = refs[2 * pages_per_step + 1:]
    c = pl.program_id(1)
    page = k_refs[0].shape[0] // SB_HEADS

    def head_rows(ref, h):
        return ref[pl.ds(h, page, stride=SB_HEADS), :]

    @pl.when(c == 0)
    def _():
        acc_ref[...] = jnp.zeros_like(acc_ref)
        cm_ref[...] = jnp.zeros_like(cm_ref)

    q = q_ref[0] * SCALE
    bias = b_ref[...]
    lane = lax.broadcasted_iota(jnp.int32, (page, LANES), 1)
    row = lax.broadcasted_iota(jnp.int32, (page, page), 0)
    col = lax.broadcasted_iota(jnp.int32, (page, page), 1)
    later = (col > row).astype(BF16)
    acc = acc_ref[...]
    cm = cm_ref[...]
    zs = []
    for k_ref in k_refs:
        z = jnp.zeros((page, LANES), F32)
        for h in range(SB_HEADS):
            zh = jnp.sum(head_rows(k_ref, h) * q[:, h * HEAD_DIM:(h + 1) * HEAD_DIM], axis=-1, keepdims=True)
            z = jnp.where(lane == h, zh, z)
        zs.append(z + bias)
    sps = [_softplus(z) for z in zs]
    afters = []
    for sp in sps:
        hi = sp.astype(BF16)
        lo = (sp - hi.astype(F32)).astype(BF16)
        afters.append(jnp.dot(later, hi, preferred_element_type=F32) + jnp.dot(later, lo, preferred_element_type=F32))
    for v_ref, z, sp, after in zip(v_refs, zs, sps, afters):
        a = jnp.exp(z - sp - after - cm)
        parts = []
        for h in range(SB_HEADS):
            av = a[:, h:h + 1] * head_rows(v_ref, h)
            parts.append(jnp.sum(av.reshape(page // SUBLANES, SUBLANES, HEAD_DIM), axis=0))
        acc = acc + jnp.concatenate(parts, axis=1)
        cm = cm + jnp.sum(sp, axis=0, keepdims=True)
    acc_ref[...] = acc
    cm_ref[...] = cm

    @pl.when(c == pl.num_programs(1) - 1)
    def _():
        o_ref[0] = jnp.sum(acc, axis=0, keepdims=True)


def _sb_step(p_b, k_pages, v_pages, page_table, bias, pages_per_step=16):
    nseq = p_b.shape[0]
    page = k_pages.shape[1]
    n_pages = page_table.shape[1]
    w = SB_HEADS * HEAD_DIM
    assert n_pages % pages_per_step == 0
    chunks = n_pages // pages_per_step
    assert k_pages.shape[2:] == (SB_HEADS, HEAD_DIM) and SB_HEADS == SUBLANES
    kp = k_pages.reshape(k_pages.shape[0], page * SB_HEADS, HEAD_DIM)
    vp = v_pages.reshape(v_pages.shape[0], page * SB_HEADS, HEAD_DIM)
    bias_row = jnp.zeros((1, LANES), F32).at[0, :SB_HEADS].set(bias)

    def page_spec(u):
        return pl.BlockSpec((None, page * SB_HEADS, HEAD_DIM),
                            lambda s, c, pt: (pt[s, n_pages - 1 - (c * pages_per_step + u)], 0, 0))

    grid_spec = pltpu.PrefetchScalarGridSpec(
        num_scalar_prefetch=1,
        grid=(nseq, chunks),
        in_specs=[pl.BlockSpec((1, 1, w), lambda s, c, pt: (s, 0, 0)),
                  pl.BlockSpec((1, LANES), lambda s, c, pt: (0, 0))]
                 + [page_spec(u) for u in range(pages_per_step)] * 2,
        out_specs=pl.BlockSpec((1, 1, w), lambda s, c, pt: (s, 0, 0)),
        scratch_shapes=[pltpu.VMEM((SUBLANES, w), F32), pltpu.VMEM((1, LANES), F32)],
    )
    q3 = p_b[:, :w].reshape(nseq, 1, w)
    out = pl.pallas_call(
        functools.partial(_sb_step_kernel, pages_per_step=pages_per_step),
        grid_spec=grid_spec,
        out_shape=jax.ShapeDtypeStruct((nseq, 1, w), F32),
        compiler_params=_params("parallel", "arbitrary"),
        name="sb_step",
    )(page_table, q3, bias_row, *([kp] * pages_per_step), *([vp] * pages_per_step))
    return out.reshape(nseq, w)


def _store_row_tiles(ref, val):
    rows = val.shape[0]
    for s in range(SUBLANES):
        ref[pl.ds(s, rows, stride=SUBLANES), :] = val[:, s * LANES:(s + 1) * LANES]


def _load_row_tiles(ref):
    rows = ref.shape[0] // SUBLANES
    return jnp.concatenate([ref[pl.ds(s, rows, stride=SUBLANES), :] for s in range(SUBLANES)], axis=1)


def _row_tile(ref, r):
    return ref.at[pl.ds(pl.multiple_of(r * SUBLANES, SUBLANES), SUBLANES), :]


def _norm_router_kernel(xp_ref, xs_ref, g_ref, wr_ref, br_ref, xn_ref, eid_ref, gate_ref, *, prompt_tiles):
    x = jnp.where(pl.program_id(0) < prompt_tiles, xp_ref[...], xs_ref[...])
    ms = jnp.mean(x * x, axis=-1, keepdims=True)
    xn = x * lax.rsqrt(ms + RMS_EPS) * g_ref[...]
    _store_row_tiles(xn_ref, xn)
    logits = jnp.dot(xn, wr_ref[...], preferred_element_type=F32, precision=lax.Precision.HIGHEST) + br_ref[...]
    lane = lax.broadcasted_iota(jnp.int32, logits.shape, 1).astype(F32)
    first = lambda hit: jnp.min(jnp.where(hit, lane, float(LANES)), axis=-1, keepdims=True)

    lg = jnp.where(lane < MOE_GROUPS, logits, -jnp.inf)
    g_max = jnp.max(lg, axis=-1, keepdims=True)
    top_g = first(lg == g_max)
    p_top = 1.0 / jnp.sum(jnp.exp(lg - g_max), axis=-1, keepdims=True)
    lo = MOE_GROUPS + top_g * EXPERTS_PER_GROUP
    le = jnp.where((lane >= lo) & (lane < lo + EXPERTS_PER_GROUP), logits, -jnp.inf)
    v1 = jnp.max(le, axis=-1, keepdims=True)
    i1 = first(le == v1)
    le2 = jnp.where(lane == i1, -jnp.inf, le)
    v2 = jnp.max(le2, axis=-1, keepdims=True)
    i2 = first(le2 == v2)
    e21 = jnp.exp(v2 - v1)
    g1 = p_top / (1.0 + e21)
    g2 = p_top * e21 / (1.0 + e21)
    eid_ref[...] = jnp.where(lane == 0.0, i1 - MOE_GROUPS, i2 - MOE_GROUPS).astype(jnp.int32)
    gate_ref[...] = jnp.where(lane == 0.0, g1, g2)


def _norm_router(h_p, h_s, g, wr, br, tm):
    n_p, d = h_p.shape
    n_s = h_s.shape[0]
    assert n_p % tm == 0 and n_s % tm == 0 and d == SUBLANES * LANES
    tp, ts = n_p // tm, n_s // tm
    return pl.pallas_call(
        functools.partial(_norm_router_kernel, prompt_tiles=tp),
        grid=(tp + ts,),
        in_specs=[pl.BlockSpec((tm, d), lambda i: (jnp.minimum(i, tp - 1), 0)),
                  pl.BlockSpec((tm, d), lambda i: (jnp.maximum(i - tp, 0), 0)),
                  pl.BlockSpec((1, d), lambda i: (0, 0)),
                  pl.BlockSpec((d, LANES), lambda i: (0, 0)),
                  pl.BlockSpec((1, LANES), lambda i: (0, 0))],
        out_specs=[pl.BlockSpec((tm * SUBLANES, LANES), lambda i: (i, 0)),
                   pl.BlockSpec((tm, LANES), lambda i: (i, 0)),
                   pl.BlockSpec((tm, LANES), lambda i: (i, 0))],
        out_shape=[jax.ShapeDtypeStruct(((n_p + n_s) * SUBLANES, LANES), F32),
                   jax.ShapeDtypeStruct((n_p + n_s, LANES), jnp.int32),
                   jax.ShapeDtypeStruct((n_p + n_s, LANES), F32)],
        compiler_params=_params("parallel"),
        name="norm_router",
    )(h_p, h_s, g.reshape(1, d), wr, br)


def _expert_kernel(be_ref, tok_ref, base_ref, nu_ref, xn_hbm, wg_ref, wu_ref, wd_ref, yr_ref,
                   xbuf, sem, wg_s, wu_s, wd_s):
    i = pl.program_id(0)
    n_used = nu_ref[0]
    rows = MOE_ROWS

    def row_copy(base, slot, r):
        tok = tok_ref[base + r]
        return pltpu.make_async_copy(_row_tile(xn_hbm, tok), _row_tile(xbuf.at[slot], r), sem.at[slot])

    def start_gather(blk, slot):
        base = base_ref[blk]

        def body(r, carry):
            row_copy(base, slot, r).start()
            return carry
        lax.fori_loop(0, rows, body, 0, unroll=GATHER_UNROLL)

    def wait_gather(blk, slot):
        base = base_ref[blk]

        def body(r, carry):
            row_copy(base, slot, r).wait()
            return carry
        lax.fori_loop(0, rows, body, 0, unroll=GATHER_UNROLL)

    @pl.when(jnp.logical_and(i == 0, n_used > 0))
    def _():
        start_gather(0, 0)

    @pl.when(i + 1 < n_used)
    def _():
        start_gather(i + 1, (i + 1) % 2)

    @pl.when(i >= n_used)
    def _():
        yr_ref[...] = jnp.zeros_like(yr_ref)

    @pl.when(i < n_used)
    def _():
        slot = i % 2
        changed = jnp.logical_or(i == 0, be_ref[i] != be_ref[jnp.maximum(i - 1, 0)])

        @pl.when(changed)
        def _():
            wg_s[...] = wg_ref[...].astype(BF16)
            wu_s[...] = wu_ref[...].astype(BF16)
            wd_s[...] = wd_ref[...].astype(BF16)

        wait_gather(i, slot)
        x = _load_row_tiles(xbuf.at[slot]).astype(BF16)
        gate = jnp.dot(x, wg_s[...], preferred_element_type=F32)
        up = jnp.dot(x, wu_s[...], preferred_element_type=F32)
        mid = (gate * jax.nn.sigmoid(gate) * up).astype(BF16)
        y = jnp.dot(mid, wd_s[...], preferred_element_type=F32)
        _store_row_tiles(yr_ref, y)


def _experts(xn_all, blk_exp, sorted_tok, blk_base, n_used, wg, wu, wd, layer):
    n_blk = blk_exp.shape[0]
    n_rows = n_blk * MOE_ROWS
    d, de = wg.shape[2:]
    grid_spec = pltpu.PrefetchScalarGridSpec(
        num_scalar_prefetch=4,
        grid=(n_blk,),
        in_specs=[pl.BlockSpec(memory_space=pl.ANY),
                  pl.BlockSpec((None, None, d, de), lambda i, be, *_: (layer, be[i], 0, 0)),
                  pl.BlockSpec((None, None, d, de), lambda i, be, *_: (layer, be[i], 0, 0)),
                  pl.BlockSpec((None, None, de, d), lambda i, be, *_: (layer, be[i], 0, 0))],
        out_specs=pl.BlockSpec((MOE_ROWS * SUBLANES, LANES), lambda i, *_: (i, 0)),
        scratch_shapes=[pltpu.VMEM((2, MOE_ROWS * SUBLANES, LANES), F32), pltpu.SemaphoreType.DMA((2,)),
                        pltpu.VMEM((d, de), BF16), pltpu.VMEM((d, de), BF16), pltpu.VMEM((de, d), BF16)],
    )
    return pl.pallas_call(
        _expert_kernel,
        grid_spec=grid_spec,
        out_shape=jax.ShapeDtypeStruct((n_rows * SUBLANES, LANES), F32),
        compiler_params=_params("arbitrary"),
        name="moe_experts",
    )(blk_exp, sorted_tok, blk_base, n_used, xn_all, wg, wu, wd)


def _combine_kernel(p0_ref, p1_ref, yr_hbm, g0_ref, g1_ref, h_ref, out_ref, buf, sem, *, tc):
    i = pl.program_id(0)
    nsteps = pl.num_programs(0)

    def row_copy(step, slot, r, which):
        pos_ref = p1_ref if which else p0_ref
        pos = pos_ref[step * tc + r]
        return pltpu.make_async_copy(_row_tile(yr_hbm, pos), _row_tile(buf.at[slot, which], r), sem.at[slot])

    def start_gather(step, slot):
        def body(r, carry):
            row_copy(step, slot, r, 0).start()
            row_copy(step, slot, r, 1).start()
            return carry
        lax.fori_loop(0, tc, body, 0, unroll=GATHER_UNROLL)

    def wait_gather(step, slot):
        def body(r, carry):
            row_copy(step, slot, r, 0).wait()
            row_copy(step, slot, r, 1).wait()
            return carry
        lax.fori_loop(0, tc, body, 0, unroll=GATHER_UNROLL)

    @pl.when(i == 0)
    def _():
        start_gather(0, 0)

    @pl.when(i + 1 < nsteps)
    def _():
        start_gather(i + 1, (i + 1) % 2)

    slot = i % 2
    wait_gather(i, slot)
    y = g0_ref[...] * _load_row_tiles(buf.at[slot, 0]) + g1_ref[...] * _load_row_tiles(buf.at[slot, 1])
    out_ref[...] = h_ref[...] + y


def _combine(yr, pos, gates, h, tc):
    n, d = h.shape
    assert n % tc == 0
    pos0, pos1 = pos[:, 0], pos[:, 1]
    g0, g1 = gates[:, 0:1], gates[:, 1:2]
    grid_spec = pltpu.PrefetchScalarGridSpec(
        num_scalar_prefetch=2,
        grid=(n // tc,),
        in_specs=[pl.BlockSpec(memory_space=pl.ANY),
                  pl.BlockSpec((tc, 1), lambda i, p0, p1: (i, 0)),
                  pl.BlockSpec((tc, 1), lambda i, p0, p1: (i, 0)),
                  pl.BlockSpec((tc, d), lambda i, p0, p1: (i, 0))],
        out_specs=pl.BlockSpec((tc, d), lambda i, p0, p1: (i, 0)),
        scratch_shapes=[pltpu.VMEM((2, 2, tc * SUBLANES, LANES), F32), pltpu.SemaphoreType.DMA((2,))],
    )
    return pl.pallas_call(
        functools.partial(_combine_kernel, tc=tc),
        grid_spec=grid_spec,
        out_shape=jax.ShapeDtypeStruct((n, d), F32),
        compiler_params=_params("arbitrary"),
        name="moe_combine",
    )(pos0, pos1, yr, g0, g1, h)


def _dispatch(eid):
    n = eid.shape[0]
    n_assign = 2 * n
    e_flat = eid.reshape(n_assign)
    onehot = (e_flat[:, None] == jnp.arange(N_EXPERTS, dtype=jnp.int32)[None, :]).astype(jnp.int32)
    csum = jnp.cumsum(onehot, axis=0)
    counts = csum[-1]
    rank = jnp.sum((csum - onehot) * onehot, axis=1)
    padded = (counts + MOE_ROWS - 1) // MOE_ROWS * MOE_ROWS
    pend = jnp.cumsum(padded)
    pstart = pend - padded
    dest = (pstart[e_flat] + rank).astype(jnp.int32)
    n_blk = (n_assign + N_EXPERTS * (MOE_ROWS - 1) + MOE_ROWS - 1) // MOE_ROWS
    blk_start = jnp.arange(n_blk, dtype=jnp.int32) * MOE_ROWS
    blk_exp = jnp.minimum(jnp.sum(blk_start[:, None] >= pend[None, :], axis=1), N_EXPERTS - 1).astype(jnp.int32)
    n_used = (pend[-1] // MOE_ROWS).astype(jnp.int32).reshape(1)
    order = jnp.argsort(e_flat, stable=True).astype(jnp.int32)
    sorted_tok = jnp.concatenate([order // 2, jnp.zeros((MOE_ROWS,), jnp.int32)])
    start = jnp.cumsum(counts) - counts
    blk_base = jnp.clip(start[blk_exp] + blk_start - pstart[blk_exp], 0, n_assign).astype(jnp.int32)
    return sorted_tok, blk_exp, blk_base, n_used, dest.reshape(n, 2)


def _hier_moe(h_p, h_s, g, wr, br, wg, wu, wd, layer):
    n_p, d = h_p.shape
    n_s = h_s.shape[0]
    xn_all, eid, gates = _norm_router(h_p, h_s, g, wr, br, math.gcd(128, n_s))
    eid, gates = eid[:, :2], gates[:, :2]
    sorted_tok, blk_exp, blk_base, n_used, pos = _dispatch(eid)
    yr = _experts(xn_all, blk_exp, sorted_tok, blk_base, n_used, wg, wu, wd, layer)
    out_p = _combine(yr, pos[:n_p], gates[:n_p], h_p, 128)
    out_s = _combine(yr, pos[n_p:], gates[n_p:], h_s, min(128, n_s))
    return out_p, out_s


def kernel(x_prompt, x_sample, mem_prompt, cache_win_k0, cache_win_v0, cache_win_k1, cache_win_v1, cache_win_k2, cache_win_v2, cache_mem_k, cache_mem_v, cache_k_pages, cache_v_pages, page_table, g_mix, g_mem, g_ffn, g_kv, g_final, w_in_a, w_out_a, w_in_b, w_out_b, b_sb, w_kv, w_mem_k, w_mem_v, w_router_group, b_router_group, w_router_expert, b_router_expert, w_exp_gate, w_exp_up, w_exp_down):
    bp, seq, d = x_prompt.shape
    bs, t_new, _ = x_sample.shape
    assert t_new == 1, "one new token per sample sequence"
    depth = g_mix.shape[0]
    n_a = w_in_a.shape[0]
    n_p = bp * seq
    tm_p = 512
    tm_s = bs
    mem_tok = mem_prompt.shape[1]
    gw = WIN_HEADS * HEAD_DIM
    win_k_cache = (cache_win_k0, cache_win_k1, cache_win_k2)
    win_v_cache = (cache_win_v0, cache_win_v1, cache_win_v2)

    h = x_prompt.reshape(n_p, d)
    hs = x_sample.reshape(bs, d)
    mem2 = mem_prompt.reshape(bp * mem_tok, d)

    wkp = [[] for _ in WIN_GROUPS]
    wvp = [[] for _ in WIN_GROUPS]
    wks = [[] for _ in WIN_GROUPS]
    wvs = [[] for _ in WIN_GROUPS]
    mkp_list, mvp_list = [], []

    for l in range(depth):
        w_mem = jnp.concatenate([w_mem_k[l], w_mem_v[l]], axis=1).astype(BF16)
        memkv = _rms_matmul(mem2, g_mem[l], w_mem, mem2.shape[0], w_mem.shape[1])
        mw = MEM_HEADS * HEAD_DIM
        mkp_list.append(memkv[:, :mw].reshape(bp, mem_tok, MEM_HEADS, HEAD_DIM))
        mvp_list.append(memkv[:, mw:].reshape(bp, mem_tok, MEM_HEADS, HEAD_DIM))
        if l < n_a:
            w_in = w_in_a[l].astype(BF16)
            pa = _rms_matmul(h, g_mix[l], w_in, tm_p, 2560)
            pa_s = _rms_matmul(hs, g_mix[l], w_in, tm_s, 2560)
            outs, lses = [], []
            for g, (win, dil) in enumerate(WIN_GROUPS):
                o, lse = _win_prompt(pa, g, seq)
                outs.append(o)
                lses.append(lse)
                keep = min(win, seq)
                tail = pa.reshape(bp, seq, -1)[:, seq - keep:]
                ka = tail[:, :, (N_WIN_GROUPS + g) * gw:(N_WIN_GROUPS + g + 1) * gw]
                va = tail[:, :, (2 * N_WIN_GROUPS + g) * gw:(2 * N_WIN_GROUPS + g + 1) * gw]
                wkp[g].append(ka.reshape(bp, keep, WIN_HEADS, HEAD_DIM))
                wvp[g].append(va.reshape(bp, keep, WIN_HEADS, HEAD_DIM))
                kn = pa_s[:, (N_WIN_GROUPS + g) * gw:(N_WIN_GROUPS + g + 1) * gw].reshape(bs, 1, WIN_HEADS, HEAD_DIM)
                vn = pa_s[:, (2 * N_WIN_GROUPS + g) * gw:(2 * N_WIN_GROUPS + g + 1) * gw].reshape(bs, 1, WIN_HEADS, HEAD_DIM)
                keep_s = min(win, win_k_cache[g].shape[2] + 1)
                wks[g].append(jnp.concatenate([win_k_cache[g][l], kn], axis=1)[:, -keep_s:])
                wvs[g].append(jnp.concatenate([win_v_cache[g][l], vn], axis=1)[:, -keep_s:])
            outs_s, lses_s = zip(*[_win_step(pa_s, g, win_k_cache[g], win_v_cache[g], l)
                                   for g in range(N_WIN_GROUPS)])
            q_mem_block = 3 * N_WIN_GROUPS
            om = _mem_prompt(pa, q_mem_block, memkv, seq, tm_p)
            om_s = _mem_step(pa_s, q_mem_block, cache_mem_k, cache_mem_v, l)
            w_out = w_out_a[l].astype(BF16)
            h = _out_a(outs, lses, om, h, w_out, tm_p)
            hs = _out_a(outs_s, lses_s, om_s, hs, w_out, tm_s)
        else:
            lb = l - n_a
            if l == n_a:
                wkv = w_kv.astype(BF16)
                k_p, v_p, kv_p16 = _kv_proj(h, g_kv, wkv, tm_p)
                k_s, v_s, _ = _kv_proj(hs, g_kv, wkv, tm_s)
            w_in = w_in_b[lb].astype(BF16)
            pb = _rms_matmul(h, g_mix[l], w_in, tm_p, w_in.shape[1])
            pb_s = _rms_matmul(hs, g_mix[l], w_in, tm_s, w_in.shape[1])
            osb = _sb_prompt(pb, kv_p16, b_sb[lb], seq)
            osb_s = _sb_step(pb_s, cache_k_pages, cache_v_pages, page_table, b_sb[lb])
            q_mem_block = SB_HEADS * HEAD_DIM // (MEM_HEADS * HEAD_DIM)
            om = _mem_prompt(pb, q_mem_block, memkv, seq, tm_p)
            om_s = _mem_step(pb_s, q_mem_block, cache_mem_k, cache_mem_v, l)
            w_out = w_out_b[lb].astype(BF16)
            h = _out_b(osb, om, h, w_out, tm_p)
            hs = _out_b(osb_s, om_s, hs, w_out, tm_s)
        wr = jnp.zeros((d, LANES), F32).at[:, :MOE_GROUPS].set(w_router_group[l])
        wr = wr.at[:, MOE_GROUPS:MOE_GROUPS + N_EXPERTS].set(w_router_expert[l])
        br = jnp.zeros((1, LANES), F32).at[0, :MOE_GROUPS].set(b_router_group[l])
        br = br.at[0, MOE_GROUPS:MOE_GROUPS + N_EXPERTS].set(b_router_expert[l])
        h, hs = _hier_moe(h, hs, g_ffn[l], wr, br, w_exp_gate, w_exp_up, w_exp_down, l)

    y_prompt = _rms(h, g_final, tm_p).reshape(bp, seq, d)
    y_sample = _rms(hs, g_final, tm_s).reshape(bs, 1, d)
    stack = lambda zs: jnp.stack(zs, axis=0)
    kv_k_p = k_p.reshape(bp, seq, SB_HEADS, HEAD_DIM)
    kv_v_p = v_p.reshape(bp, seq, SB_HEADS, HEAD_DIM)
    kv_k_s = k_s.reshape(bs, 1, SB_HEADS, HEAD_DIM)
    kv_v_s = v_s.reshape(bs, 1, SB_HEADS, HEAD_DIM)
    return (y_prompt, y_sample,
            stack(wkp[0]), stack(wvp[0]), stack(wkp[1]), stack(wvp[1]), stack(wkp[2]), stack(wvp[2]),
            stack(wks[0]), stack(wvs[0]), stack(wks[1]), stack(wvs[1]), stack(wks[2]), stack(wvs[2]),
            stack(mkp_list), stack(mvp_list),
            kv_k_p, kv_v_p, kv_k_s, kv_v_s)
```
